```python
import functools
import jax
import jax.numpy as jnp
from jax import lax
import numpy as np

D_MODEL = 4096
BATCH = 4
SEQ = 2048
DEPTH = 2
DEC_BATCH = 128
DEC_SEQ = 1
PAST_LEN = 16384
PAGE_SIZE = 128

POOL_WINDOWS = (2, 4, 8, 16)
N_POOL_GROUPS = len(POOL_WINDOWS)
POOL_WIDTH = D_MODEL // 4
POOL_GROUP = POOL_WIDTH // N_POOL_GROUPS
POOL_BUF = max(POOL_WINDOWS) - 1
LRU_WIDTH = D_MODEL // 4
LRU_HEADS = 16
LRU_BLOCK = LRU_WIDTH // LRU_HEADS
CONV_WIDTH = 4
LRU_C = 8.0
MLA_HEADS = 16
QK_NOPE = 128
QK_ROPE = 64
QK_HEAD = QK_NOPE + QK_ROPE
V_HEAD = 128
Q_LORA = D_MODEL // 4
KV_LORA = 512
MLA_WIDTH = MLA_HEADS * V_HEAD
ROPE_BASE = 10000.0
ATTN_SCALE = QK_HEAD ** -0.5
Q_BLOCK = 128
N_BRANCHES = 3
IN_WIDTH = POOL_WIDTH + LRU_WIDTH + Q_LORA + KV_LORA + QK_ROPE + N_BRANCHES * D_MODEL
D_FF = ((8 * D_MODEL // 3 + 255) // 256) * 256
N_EXPERTS = 8
TOP_K = 2
D_FF_EXPERT = D_MODEL
N_DENSE = (DEPTH + 1) // 2
N_MOE = DEPTH // 2
EPS = 1e-6

kernel_name = 'hybrid_pool_lru_mla_decoder_step'


def rms_norm(x, g):
    xf = x.astype(jnp.float32)
    y = xf * lax.rsqrt(jnp.mean(xf * xf, axis=-1, keepdims=True) + EPS)
    return (y * g.astype(jnp.float32)).astype(x.dtype)


def rope(x, pos):
    half = QK_ROPE // 2
    inv = ROPE_BASE ** (-jnp.arange(half, dtype=jnp.float32) / half)
    ang = pos.astype(jnp.float32)[:, None, None] * inv
    cos, sin = jnp.cos(ang), jnp.sin(ang)
    x1 = x[..., :half].astype(jnp.float32)
    x2 = x[..., half:].astype(jnp.float32)
    out = jnp.concatenate([x1 * cos - x2 * sin, x2 * cos + x1 * sin], axis=-1)
    return out.astype(x.dtype)


def pool_mix(u_full, pos_full, pool_w, pool_scale):
    B, L, _ = u_full.shape
    uf = u_full.astype(jnp.float32)
    cs = jnp.cumsum(uf, axis=1)
    groups = []
    for g, win in enumerate(POOL_WINDOWS):
        lo, hi = g * POOL_GROUP, (g + 1) * POOL_GROUP
        csg = cs[..., lo:hi]
        prev = jnp.pad(csg, ((0, 0), (win, 0), (0, 0)))[:, :L]
        cnt = jnp.minimum(pos_full + 1, win).astype(jnp.float32)[None, :, None]
        groups.append((csg - prev) / cnt - uf[..., lo:hi])
    pooled = jnp.stack(groups, axis=2).astype(u_full.dtype)
    mixed = jnp.einsum('blgc,gcd->blgd', pooled, pool_w).reshape(B, L, POOL_WIDTH)
    return mixed * pool_scale


def rg_lru(xc, h0, wa, ba, wx, bx, lam):
    B, T, _ = xc.shape
    xb = xc.reshape(B, T, LRU_HEADS, LRU_BLOCK)
    r = jax.nn.sigmoid(jnp.einsum('bthi,hij->bthj', xb, wa).reshape(B, T, LRU_WIDTH) + ba)
    gi = jax.nn.sigmoid(jnp.einsum('bthi,hij->bthj', xb, wx).reshape(B, T, LRU_WIDTH) + bx)
    log_a = -LRU_C * r.astype(jnp.float32) * jax.nn.softplus(-lam.astype(jnp.float32))
    a = jnp.exp(log_a)
    b_in = jnp.sqrt(-jnp.expm1(2.0 * log_a)) * (gi * xc).astype(jnp.float32)

    def step(h, ab):
        a_t, b_t = ab
        h = a_t * h + b_t
        return h, h

    h_last, hs = lax.scan(step, h0.astype(jnp.float32), (a.swapaxes(0, 1), b_in.swapaxes(0, 1)))
    return hs.swapaxes(0, 1).astype(xc.dtype), h_last


def mla_prompt_attend(q_nope, q_rope, ckv, kr, w_ukv):
    B, T = q_nope.shape[:2]
    kv = jnp.einsum('btc,cf->btf', ckv, w_ukv).reshape(B, T, MLA_HEADS, QK_NOPE + V_HEAD)
    k_nope, v = kv[..., :QK_NOPE], kv[..., QK_NOPE:]
    nb = T // Q_BLOCK
    qn_b = q_nope.reshape(B, nb, Q_BLOCK, MLA_HEADS, QK_NOPE).swapaxes(0, 1)
    qr_b = q_rope.reshape(B, nb, Q_BLOCK, MLA_HEADS, QK_ROPE).swapaxes(0, 1)
    kpos = jnp.arange(T)

    def block(args):
        qn, qr, bi = args
        s = (jnp.einsum('bqhd,bkhd->bhqk', qn, k_nope).astype(jnp.float32)
             + jnp.einsum('bqhr,bkr->bhqk', qr, kr).astype(jnp.float32)) * ATTN_SCALE
        qpos = bi * Q_BLOCK + jnp.arange(Q_BLOCK)
        s = jnp.where(kpos[None, :] <= qpos[:, None], s, -jnp.inf)
        p = jax.nn.softmax(s, axis=-1).astype(v.dtype)
        return jnp.einsum('bhqk,bkhd->bqhd', p, v)

    o = lax.map(block, (qn_b, qr_b, jnp.arange(nb)))
    return o.swapaxes(0, 1).reshape(B, T, MLA_WIDTH)


def mla_sample_attend(q_nope, q_rope, ckv, kr, w_ukv, cache_ckv, cache_krope, page_table, layer):
    Bd, S = q_nope.shape[:2]
    w = w_ukv.reshape(KV_LORA, MLA_HEADS, QK_NOPE + V_HEAD)
    w_uk, w_uv = w[..., :QK_NOPE], w[..., QK_NOPE:]
    q_lat = jnp.einsum('bshd,chd->bshc', q_nope, w_uk)

    def update(carry, ck, kr_blk, mask):
        m, l, acc = carry
        s = (jnp.einsum('bshc,bkc->bhsk', q_lat, ck).astype(jnp.float32)
             + jnp.einsum('bshr,bkr->bhsk', q_rope, kr_blk).astype(jnp.float32)) * ATTN_SCALE
        if mask is not None:
            s = jnp.where(mask, s, -jnp.inf)
        m_new = jnp.maximum(m, jnp.max(s, axis=-1))
        corr = jnp.exp(m - m_new)
        p = jnp.exp(s - m_new[..., None])
        l = l * corr + jnp.sum(p, axis=-1)
        acc = acc * corr[..., None] + jnp.einsum('bhsk,bkc->bhsc', p, ck.astype(jnp.float32))
        return m_new, l, acc

    init = (jnp.full((Bd, MLA_HEADS, S), -jnp.inf, jnp.float32),
            jnp.zeros((Bd, MLA_HEADS, S), jnp.float32),
            jnp.zeros((Bd, MLA_HEADS, S, KV_LORA), jnp.float32))

    def page_step(carry, phys):
        ck = cache_ckv[layer, phys]
        kp = cache_krope[layer, phys]
        return update(carry, ck, kp, None), None

    carry, _ = lax.scan(page_step, init, page_table.T)
    causal = jnp.arange(S)[None, :] <= jnp.arange(S)[:, None]
    m, l, acc = update(carry, ckv, kr, causal[None, None])
    o_lat = (acc / l[..., None]).astype(w_uv.dtype)
    return jnp.einsum('bhsc,chd->bshd', o_lat, w_uv).reshape(Bd, S, MLA_WIDTH)


def token_mixers(h, start, pool_prev, conv_prev, h0, attend,
                 w_in, pool_w, pool_scale, conv_w, conv_b, wa, ba, wx, bx, lam,
                 q_norm, w_uq, kv_norm, wb_pool, wb_lru, wb_mla, w_out):
    B, T, _ = h.shape
    pos = start + jnp.arange(T)
    z = jnp.einsum('btd,df->btf', h, w_in)
    o1 = POOL_WIDTH
    o2 = o1 + LRU_WIDTH
    o3 = o2 + Q_LORA
    o4 = o3 + KV_LORA
    o5 = o4 + QK_ROPE
    u_pool, u_lru, c_q, c_kv, k_r, gate_logits = (z[..., :o1], z[..., o1:o2], z[..., o2:o3],
                                                  z[..., o3:o4], z[..., o4:o5], z[..., o5:])
    P = pool_prev.shape[1]
    u_full = jnp.concatenate([pool_prev.astype(u_pool.dtype), u_pool], axis=1)
    pos_full = start - P + jnp.arange(P + T)
    o_pool = pool_mix(u_full, pos_full, pool_w, pool_scale)[:, P:]
    new_pool = u_full[:, -POOL_BUF:]
    conv_full = jnp.concatenate([conv_prev.astype(u_lru.dtype), u_lru], axis=1)
    xc = conv_b + conv_w[0] * conv_full[:, 0:T]
    for k in range(1, CONV_WIDTH):
        xc = xc + conv_w[k] * conv_full[:, k:k + T]
    o_lru, h_new = rg_lru(xc, h0, wa, ba, wx, bx, lam)
    new_conv = conv_full[:, -(CONV_WIDTH - 1):]
    cq = rms_norm(c_q, q_norm)
    q = jnp.einsum('btc,cf->btf', cq, w_uq).reshape(B, T, MLA_HEADS, QK_HEAD)
    q_nope = q[..., :QK_NOPE]
    q_rope = rope(q[..., QK_NOPE:], pos)
    ckv = rms_norm(c_kv, kv_norm)
    kr = rope(k_r[:, :, None, :], pos)[:, :, 0]
    o_mla = attend(q_nope, q_rope, ckv, kr)
    g = jax.nn.sigmoid(gate_logits).reshape(B, T, N_BRANCHES, D_MODEL)
    merged = (g[:, :, 0] * jnp.einsum('btc,cd->btd', o_pool, wb_pool)
              + g[:, :, 1] * jnp.einsum('btc,cd->btd', o_lru, wb_lru)
              + g[:, :, 2] * jnp.einsum('btc,cd->btd', o_mla, wb_mla))
    out = jnp.einsum('btd,de->bte', merged, w_out)
    return out, ckv, kr, new_pool, new_conv, h_new


def swiglu(h, w_gate, w_up, w_down):
    a = jnp.einsum('btd,df->btf', h, w_gate)
    b = jnp.einsum('btd,df->btf', h, w_up)
    return jnp.einsum('btf,fd->btd', jax.nn.silu(a) * b, w_down)


def moe_ffn(h, w_router, b_router, w_gate, w_up, w_down):
    logits = (jnp.einsum('btd,de->bte', h, w_router) + b_router).astype(jnp.float32)
    top_v, top_i = lax.top_k(logits, TOP_K)
    top_w = jax.nn.softmax(top_v, axis=-1)
    combine = jnp.sum(jax.nn.one_hot(top_i, N_EXPERTS, dtype=jnp.float32) * top_w[..., None], axis=-2)
    y = jnp.zeros_like(h)
    for e in range(N_EXPERTS):
        y = y + combine[..., e:e + 1].astype(h.dtype) * swiglu(h, w_gate[e], w_up[e], w_down[e])
    return y


def setup_inputs(seed: int = 0) -> dict:
    key = jax.random.key(seed)
    ks = iter(jax.random.split(key, 64))

    def nrm(shape, scale):
        return jax.random.normal(next(ks), shape, jnp.float32) * scale

    def gain(shape):
        return 1.0 + 0.05 * jax.random.normal(next(ks), shape, jnp.float32)

    n_pages = PAST_LEN // PAGE_SIZE
    n_used = DEC_BATCH * n_pages
    n_phys = n_used + n_used // 4
    perm = jax.random.permutation(next(ks), n_phys)
    page_table = perm[:n_used].reshape(DEC_BATCH, n_pages).astype(jnp.int32)

    s_ = jax.random.uniform(next(ks), (DEPTH, LRU_WIDTH), jnp.float32, 0.9, 0.999) ** (1.0 / LRU_C)
    lru_lambda = jnp.log(s_) - jnp.log1p(-s_)

    return {
        'x_prompt': nrm((BATCH, SEQ, D_MODEL), 1.0),
        'x_sample': nrm((DEC_BATCH, DEC_SEQ, D_MODEL), 1.0),
        'cache_ckv': nrm((DEPTH, n_phys, PAGE_SIZE, KV_LORA), 1.0),
        'cache_krope': nrm((DEPTH, n_phys, PAGE_SIZE, QK_ROPE), 1.0),
        'state_pool': nrm((DEPTH, DEC_BATCH, POOL_BUF, POOL_WIDTH), 1.0),
        'state_lru_conv': nrm((DEPTH, DEC_BATCH, CONV_WIDTH - 1, LRU_WIDTH), 1.0),
        'state_lru_h': nrm((DEPTH, DEC_BATCH, LRU_WIDTH), 0.5),
        'page_table': page_table,
        'norm_mix_pre': gain((DEPTH, D_MODEL)),
        'norm_mix_post': gain((DEPTH, D_MODEL)),
        'norm_ffn_pre': gain((DEPTH, D_MODEL)),
        'norm_ffn_post': gain((DEPTH, D_MODEL)),
        'w_in': nrm((DEPTH, D_MODEL, IN_WIDTH), D_MODEL ** -0.5),
        'pool_w': nrm((DEPTH, N_POOL_GROUPS, POOL_GROUP, POOL_GROUP), POOL_GROUP ** -0.5),
        'pool_scale': gain((DEPTH, POOL_WIDTH)),
        'lru_conv_w': nrm((DEPTH, CONV_WIDTH, LRU_WIDTH), CONV_WIDTH ** -0.5),
        'lru_conv_b': nrm((DEPTH, LRU_WIDTH), 0.01),
        'lru_wa': nrm((DEPTH, LRU_HEADS, LRU_BLOCK, LRU_BLOCK), LRU_BLOCK ** -0.5),
        'lru_ba': nrm((DEPTH, LRU_WIDTH), 0.01),
        'lru_wx': nrm((DEPTH, LRU_HEADS, LRU_BLOCK, LRU_BLOCK), LRU_BLOCK ** -0.5),
        'lru_bx': nrm((DEPTH, LRU_WIDTH), 0.01),
        'lru_lambda': lru_lambda,
        'mla_q_norm': gain((DEPTH, Q_LORA)),
        'mla_w_uq': nrm((DEPTH, Q_LORA, MLA_HEADS * QK_HEAD), Q_LORA ** -0.5),
        'mla_kv_norm': gain((DEPTH, KV_LORA)),
        'mla_w_ukv': nrm((DEPTH, KV_LORA, MLA_HEADS * (QK_NOPE + V_HEAD)), KV_LORA ** -0.5),
        'w_branch_pool': nrm((DEPTH, POOL_WIDTH, D_MODEL), POOL_WIDTH ** -0.5),
        'w_branch_lru': nrm((DEPTH, LRU_WIDTH, D_MODEL), LRU_WIDTH ** -0.5),
        'w_branch_mla': nrm((DEPTH, MLA_WIDTH, D_MODEL), MLA_WIDTH ** -0.5),
        'w_out': nrm((DEPTH, D_MODEL, D_MODEL), D_MODEL ** -0.5),
        'ffn_w_gate': nrm((N_DENSE, D_MODEL, D_FF), D_MODEL ** -0.5),
        'ffn_w_up': nrm((N_DENSE, D_MODEL, D_FF), D_MODEL ** -0.5),
        'ffn_w_down': nrm((N_DENSE, D_FF, D_MODEL), D_FF ** -0.5),
        'moe_w_router': nrm((N_MOE, D_MODEL, N_EXPERTS), D_MODEL ** -0.5),
        'moe_b_router': nrm((N_MOE, N_EXPERTS), 0.01),
        'moe_w_gate': nrm((N_MOE, N_EXPERTS, D_MODEL, D_FF_EXPERT), D_MODEL ** -0.5),
        'moe_w_up': nrm((N_MOE, N_EXPERTS, D_MODEL, D_FF_EXPERT), D_MODEL ** -0.5),
        'moe_w_down': nrm((N_MOE, N_EXPERTS, D_FF_EXPERT, D_MODEL), D_FF_EXPERT ** -0.5),
    }


def reference(x_prompt, x_sample, cache_ckv, cache_krope, state_pool, state_lru_conv, state_lru_h,
              page_table, norm_mix_pre, norm_mix_post, norm_ffn_pre, norm_ffn_post, w_in,
              pool_w, pool_scale, lru_conv_w, lru_conv_b, lru_wa, lru_ba, lru_wx, lru_bx, lru_lambda,
              mla_q_norm, mla_w_uq, mla_kv_norm, mla_w_ukv, w_branch_pool, w_branch_lru, w_branch_mla,
              w_out, ffn_w_gate, ffn_w_up, ffn_w_down, moe_w_router, moe_b_router, moe_w_gate,
              moe_w_up, moe_w_down):
    B = x_prompt.shape[0]
    xp, xs = x_prompt, x_sample
    st_prompt = [[], [], [], [], []]
    st_sample = [[], [], [], [], []]
    for i in range(DEPTH):
        def run_group(x, start, pool_prev, conv_prev, h0, attend):
            h = rms_norm(x, norm_mix_pre[i])
            mix, *new_state = token_mixers(
                h, start, pool_prev, conv_prev, h0, attend,
                w_in[i], pool_w[i], pool_scale[i], lru_conv_w[i], lru_conv_b[i],
                lru_wa[i], lru_ba[i], lru_wx[i], lru_bx[i], lru_lambda[i],
                mla_q_norm[i], mla_w_uq[i], mla_kv_norm[i],
                w_branch_pool[i], w_branch_lru[i], w_branch_mla[i], w_out[i])
            x = x + rms_norm(mix, norm_mix_post[i])
            h2 = rms_norm(x, norm_ffn_pre[i])
            j = i // 2
            if i % 2 == 0:
                f = swiglu(h2, ffn_w_gate[j], ffn_w_up[j], ffn_w_down[j])
            else:
                f = moe_ffn(h2, moe_w_router[j], moe_b_router[j], moe_w_gate[j], moe_w_up[j], moe_w_down[j])
            return x + rms_norm(f, norm_ffn_post[i]), new_state

        prompt_attend = functools.partial(mla_prompt_attend, w_ukv=mla_w_ukv[i])
        sample_attend = functools.partial(mla_sample_attend, w_ukv=mla_w_ukv[i], cache_ckv=cache_ckv,
                                          cache_krope=cache_krope, page_table=page_table, layer=i)
        xp, new_p = run_group(xp, 0,
                              jnp.zeros((B, 0, POOL_WIDTH), xp.dtype),
                              jnp.zeros((B, CONV_WIDTH - 1, LRU_WIDTH), xp.dtype),
                              jnp.zeros((B, LRU_WIDTH), jnp.float32),
                              prompt_attend)
        xs, new_s = run_group(xs, PAST_LEN, state_pool[i], state_lru_conv[i], state_lru_h[i], sample_attend)
        for lst, v in zip(st_prompt, new_p):
            lst.append(v)
        for lst, v in zip(st_sample, new_s):
            lst.append(v)
    new_ckv_prompt = jnp.stack(st_prompt[0])
    new_krope_prompt = jnp.stack(st_prompt[1])
    new_pool_prompt = jnp.stack(st_prompt[2])
    new_conv_prompt = jnp.stack(st_prompt[3])
    new_h_prompt = jnp.stack(st_prompt[4])
    new_ckv_sample = jnp.stack(st_sample[0])
    new_krope_sample = jnp.stack(st_sample[1])
    new_pool_sample = jnp.stack(st_sample[2])
    new_conv_sample = jnp.stack(st_sample[3])
    new_h_sample = jnp.stack(st_sample[4])
    return (xp, xs, new_ckv_prompt, new_krope_prompt, new_pool_prompt, new_conv_prompt, new_h_prompt,
            new_ckv_sample, new_krope_sample, new_pool_sample, new_conv_sample, new_h_sample)
```

```python
import functools

import jax
import jax.numpy as jnp
from jax import lax
from jax.experimental import pallas as pl
from jax.experimental.pallas import tpu as pltpu

BF16 = jnp.bfloat16
F32 = jnp.float32

V7X_LANES = 128
V7X_BF16_SUBLANES = 16
V7X_VMEM_LIMIT_BYTES = 56 * 1024 * 1024

POOL_WINDOWS = (2, 4, 8, 16)
LRU_C = 8.0
QK_NOPE = 128
QK_ROPE = 64
V_HEAD = 128
ROPE_BASE = 10000.0
TOP_K = 2
EPS = 1e-6
MXU_TILE = 256


def _tile(n, target, mult):
    best = None
    for t in range(mult, min(n, target) + 1, mult):
        if n % t == 0:
            best = t
    return best if best is not None else n


def _params(sem):
    return pltpu.CompilerParams(dimension_semantics=sem, vmem_limit_bytes=V7X_VMEM_LIMIT_BYTES)


def _wspec(w, k, tn, prefix, col0):
    nlead = len(prefix)
    return pl.BlockSpec((None,) * nlead + (k, tn), lambda n, m: tuple(prefix) + (0, n + col0))


def _rms(x, g):
    return x * lax.rsqrt(jnp.mean(x * x, axis=-1, keepdims=True) + EPS) * g


def _norm_kernel(x_ref, g_ref, h_ref):
    h_ref[...] = _rms(x_ref[...], g_ref[...]).astype(h_ref.dtype)


def _resid_norm_kernel(x_ref, f_ref, gp_ref, gn_ref, xo_ref, h_ref):
    x = x_ref[...] + _rms(f_ref[...], gp_ref[...])
    xo_ref[...] = x
    h_ref[...] = _rms(x, gn_ref[...]).astype(h_ref.dtype)


def _resid_kernel(x_ref, f_ref, gp_ref, xo_ref):
    xo_ref[...] = x_ref[...] + _rms(f_ref[...], gp_ref[...])


def _norm(x, g):
    m, d = x.shape
    tm = _tile(m, 256, V7X_BF16_SUBLANES)
    row = pl.BlockSpec((tm, d), lambda i: (i, 0))
    vec = pl.BlockSpec((1, d), lambda i: (0, 0))
    return pl.pallas_call(
        _norm_kernel, grid=(m // tm,), in_specs=[row, vec], out_specs=row,
        out_shape=jax.ShapeDtypeStruct((m, d), BF16), compiler_params=_params(("arbitrary",)),
        name="rms_norm")(x, g.reshape(1, d))


def _resid_norm(x, f, g_post, g_next):
    m, d = x.shape
    tm = _tile(m, 256, V7X_BF16_SUBLANES)
    row = pl.BlockSpec((tm, d), lambda i: (i, 0))
    vec = pl.BlockSpec((1, d), lambda i: (0, 0))
    if g_next is None:
        return pl.pallas_call(
            _resid_kernel, grid=(m // tm,), in_specs=[row, row, vec], out_specs=row,
            out_shape=jax.ShapeDtypeStruct((m, d), F32), compiler_params=_params(("arbitrary",)),
            name="resid")(x, f, g_post.reshape(1, d)), None
    return pl.pallas_call(
        _resid_norm_kernel, grid=(m // tm,), in_specs=[row, row, vec, vec], out_specs=[row, row],
        out_shape=[jax.ShapeDtypeStruct((m, d), F32), jax.ShapeDtypeStruct((m, d), BF16)],
        compiler_params=_params(("arbitrary",)),
        name="resid_norm")(x, f, g_post.reshape(1, d), g_next.reshape(1, d))


def _rope_tile(acc, cos_ref, sin_ref):
    tn = acc.shape[1]
    reps = tn // V7X_LANES
    cos = jnp.tile(cos_ref[...], (1, reps))
    sin = jnp.tile(sin_ref[...], (1, reps))
    lane = lax.broadcasted_iota(jnp.int32, acc.shape, 1) % QK_ROPE
    half = QK_ROPE // 2
    partner = jnp.where(lane < half, pltpu.roll(acc, tn - half, 1), pltpu.roll(acc, half, 1))
    return acc * cos + partner * sin


def _mm_kernel(*refs, epilogue):
    if epilogue == "rope":
        x_ref, w_ref, cos_ref, sin_ref, o_ref, wb_ref = refs
    elif epilogue == "scale_add":
        x_ref, w_ref, s_ref, a_ref, o_ref, wb_ref = refs
    else:
        x_ref, w_ref, o_ref, wb_ref = refs

    @pl.when(pl.program_id(1) == 0)
    def _():
        wb_ref[...] = w_ref[...].astype(BF16)

    acc = jnp.dot(x_ref[...], wb_ref[...], preferred_element_type=F32)
    if epilogue == "rope":
        acc = _rope_tile(acc, cos_ref, sin_ref)
    elif epilogue == "scale_add":
        acc = a_ref[...] + s_ref[...] * acc
    o_ref[...] = acc.astype(o_ref.dtype)


def _mm(x, w, *, prefix=(), col0=0, ncols, tn, tm, out_dtype, rows=None, row0=0,
        epilogue="none", extra=(), name="mm"):
    m, k = x.shape
    rows = m if rows is None else rows
    assert rows % tm == 0 and ncols % tn == 0
    grid = (ncols // tn, rows // tm)
    in_specs = [pl.BlockSpec((tm, k), lambda n, i: (i + row0, 0)), _wspec(w, k, tn, prefix, col0)]
    if epilogue == "rope":
        in_specs += [pl.BlockSpec((tm, V7X_LANES), lambda n, i: (i + row0, 0))] * 2
    elif epilogue == "scale_add":
        in_specs += [pl.BlockSpec((tm, 1), lambda n, i: (i, 0)),
                     pl.BlockSpec((tm, tn), lambda n, i: (i, n))]
    return pl.pallas_call(
        functools.partial(_mm_kernel, epilogue=epilogue), grid=grid, in_specs=in_specs,
        out_specs=pl.BlockSpec((tm, tn), lambda n, i: (i, n)),
        out_shape=jax.ShapeDtypeStruct((rows, ncols), out_dtype),
        scratch_shapes=[pltpu.VMEM((k, tn), BF16)],
        compiler_params=_params(("arbitrary", "arbitrary")), name=name)(x, w, *extra)


def _swiglu_kernel(x_ref, wg_ref, wu_ref, o_ref, wgb_ref, wub_ref):
    @pl.when(pl.program_id(1) == 0)
    def _():
        wgb_ref[...] = wg_ref[...].astype(BF16)
        wub_ref[...] = wu_ref[...].astype(BF16)

    x = x_ref[...]
    a = jnp.dot(x, wgb_ref[...], preferred_element_type=F32)
    b = jnp.dot(x, wub_ref[...], preferred_element_type=F32)
    o_ref[...] = (a * jax.nn.sigmoid(a) * b).astype(o_ref.dtype)


def _swiglu_up(x, wg, wu, prefix, *, tn, tm):
    m, k = x.shape
    n = wg.shape[-1]
    return pl.pallas_call(
        _swiglu_kernel, grid=(n // tn, m // tm),
        in_specs=[pl.BlockSpec((tm, k), lambda j, i: (i, 0)),
                  _wspec(wg, k, tn, prefix, 0), _wspec(wu, k, tn, prefix, 0)],
        out_specs=pl.BlockSpec((tm, tn), lambda j, i: (i, j)),
        out_shape=jax.ShapeDtypeStruct((m, n), BF16),
        scratch_shapes=[pltpu.VMEM((k, tn), BF16), pltpu.VMEM((k, tn), BF16)],
        compiler_params=_params(("arbitrary", "arbitrary")), name="swiglu_up")(x, wg, wu)


def _merge_kernel(op_ref, ol_ref, om_ref, wp_ref, wl_ref, wm_ref, gp_ref, gl_ref, gm_ref,
                  o_ref, wpb_ref, wlb_ref, wmb_ref):
    @pl.when(pl.program_id(1) == 0)
    def _():
        wpb_ref[...] = wp_ref[...].astype(BF16)
        wlb_ref[...] = wl_ref[...].astype(BF16)
        wmb_ref[...] = wm_ref[...].astype(BF16)

    acc = jax.nn.sigmoid(gp_ref[...]) * jnp.dot(op_ref[...], wpb_ref[...], preferred_element_type=F32)
    acc += jax.nn.sigmoid(gl_ref[...]) * jnp.dot(ol_ref[...], wlb_ref[...], preferred_element_type=F32)
    acc += jax.nn.sigmoid(gm_ref[...]) * jnp.dot(om_ref[...], wmb_ref[...], preferred_element_type=F32)
    o_ref[...] = acc.astype(o_ref.dtype)


def _merge(o_pool, o_lru, o_mla, wbp, wbl, wbm, gate_logits, layer, *, tn, tm):
    m = o_pool.shape[0]
    d = wbp.shape[-1]
    nb = d // tn
    lhs = lambda a: pl.BlockSpec((tm, a.shape[1]), lambda n, i: (i, 0))
    wsp = lambda w: _wspec(w, w.shape[-2], tn, (layer,), 0)
    gate = lambda g: pl.BlockSpec((tm, tn), lambda n, i: (i, n + g * nb))
    return pl.pallas_call(
        _merge_kernel, grid=(nb, m // tm),
        in_specs=[lhs(o_pool), lhs(o_lru), lhs(o_mla), wsp(wbp), wsp(wbl), wsp(wbm),
                  gate(0), gate(1), gate(2)],
        out_specs=pl.BlockSpec((tm, tn), lambda n, i: (i, n)),
        out_shape=jax.ShapeDtypeStruct((m, d), BF16),
        scratch_shapes=[pltpu.VMEM((wbp.shape[-2], tn), BF16), pltpu.VMEM((wbl.shape[-2], tn), BF16),
                        pltpu.VMEM((wbm.shape[-2], tn), BF16)],
        compiler_params=_params(("arbitrary", "arbitrary")),
        name="gated_merge")(o_pool, o_lru, o_mla, wbp, wbl, wbm, gate_logits, gate_logits, gate_logits)


def _pool_project(sums, u, cnts, pw_ref, ps_ref, group):
    outs = []
    for g in range(len(POOL_WINDOWS)):
        sl = slice(g * group, (g + 1) * group)
        pooled = sums[g][:, sl] / cnts[g] - u[:, sl]
        outs.append(jnp.dot(pooled.astype(BF16), pw_ref[g].astype(BF16), preferred_element_type=F32))
    return jnp.concatenate(outs, axis=1) * ps_ref[...]


def _pool_prompt_kernel(u_ref, halo_ref, pw_ref, ps_ref, o_ref, *, tt, group):
    t = pl.program_id(1)
    halo_rows = halo_ref.shape[0]
    halo = jnp.where(t == 0, 0.0, halo_ref[...])
    u = u_ref[...]
    full = jnp.concatenate([halo, u], axis=0)
    sums = []
    s = full
    span = 1
    for win in POOL_WINDOWS:
        while span < win:
            s = s + pltpu.roll(s, span, 0)
            span *= 2
        sums.append(s[halo_rows:])
    pos = t * tt + lax.broadcasted_iota(jnp.int32, (tt, 1), 0)
    cnts = [jnp.minimum(pos + 1, win).astype(F32) for win in POOL_WINDOWS]
    o_ref[...] = _pool_project(sums, u, cnts, pw_ref, ps_ref, group).astype(o_ref.dtype)


def _pool_prompt(z, pool_w, pool_scale, layer, batch, seq, width):
    tt = _tile(seq, 256, V7X_BF16_SUBLANES)
    halo = 16
    nt = seq // tt
    group = width // len(POOL_WINDOWS)
    return pl.pallas_call(
        functools.partial(_pool_prompt_kernel, tt=tt, group=group), grid=(batch, nt),
        in_specs=[pl.BlockSpec((tt, width), lambda b, t: (b * nt + t, 0)),
                  pl.BlockSpec((halo, width),
                               lambda b, t: (jnp.maximum((b * nt + t) * (tt // halo) - 1, 0), 0)),
                  pl.BlockSpec((None,) + pool_w.shape[1:], lambda b, t: (layer, 0, 0, 0)),
                  pl.BlockSpec((None, 1, width), lambda b, t: (layer, 0, 0))],
        out_specs=pl.BlockSpec((tt, width), lambda b, t: (b * nt + t, 0)),
        out_shape=jax.ShapeDtypeStruct((batch * seq, width), BF16),
        compiler_params=_params(("arbitrary", "arbitrary")), name="pool_prompt")(z, z, pool_w, pool_scale)


def _pool_sample_kernel(u_ref, st_ref, pw_ref, ps_ref, o_ref, *, group):
    u = u_ref[...]
    nbuf = st_ref.shape[0]
    sums = []
    s = u
    taken = 0
    for win in POOL_WINDOWS:
        while taken < win - 1:
            taken += 1
            s = s + st_ref[nbuf - taken]
        sums.append(s)
    cnts = [float(win) for win in POOL_WINDOWS]
    o_ref[...] = _pool_project(sums, u, cnts, pw_ref, ps_ref, group).astype(o_ref.dtype)


def _pool_sample(z, state_t, pool_w, pool_scale, layer, row_block, nrows, width):
    group = width // len(POOL_WINDOWS)
    return pl.pallas_call(
        functools.partial(_pool_sample_kernel, group=group), grid=(1,),
        in_specs=[pl.BlockSpec((nrows, width), lambda i: (row_block, 0)),
                  pl.BlockSpec(state_t.shape, lambda i: (0, 0, 0)),
                  pl.BlockSpec((None,) + pool_w.shape[1:], lambda i: (layer, 0, 0, 0)),
                  pl.BlockSpec((None, 1, width), lambda i: (layer, 0, 0))],
        out_specs=pl.BlockSpec((nrows, width), lambda i: (0, 0)),
        out_shape=jax.ShapeDtypeStruct((nrows, width), BF16),
        compiler_params=_params(("arbitrary",)), name="pool_sample")(z, state_t, pool_w, pool_scale)


def _lru_gates(xc, wa_ref, wx_ref, ba_ref, bx_ref, lam_ref):
    nchunk, cw, _ = wa_ref.shape
    xb = xc.astype(BF16)
    ra, rx = [], []
    for c in range(nchunk):
        xs = xb[:, c * cw:(c + 1) * cw]
        ra.append(jnp.dot(xs, wa_ref[c].astype(BF16), preferred_element_type=F32))
        rx.append(jnp.dot(xs, wx_ref[c].astype(BF16), preferred_element_type=F32))
    r = jax.nn.sigmoid(jnp.concatenate(ra, axis=1) + ba_ref[...])
    gi = jax.nn.sigmoid(jnp.concatenate(rx, axis=1) + bx_ref[...])
    nl = -lam_ref[...]
    softplus = jnp.maximum(nl, 0.0) + jnp.log1p(jnp.exp(-jnp.abs(nl)))
    log_a = -LRU_C * r * softplus
    a = jnp.exp(log_a)
    b = jnp.sqrt(jnp.tanh(-log_a) * (a * a + 1.0)) * (gi * xc)
    return a, b


def _lru_prompt_kernel(u_ref, cw_ref, cb_ref, wa_ref, wx_ref, ba_ref, bx_ref, lam_ref,
                       o_ref, hl_ref, tail_ref, h_ref, *, tc):
    t = pl.program_id(1)
    keep = tail_ref.shape[0]

    @pl.when(t == 0)
    def _():
        tail_ref[...] = jnp.zeros_like(tail_ref)
        h_ref[...] = jnp.zeros_like(h_ref)

    u = u_ref[...]
    full = jnp.concatenate([tail_ref[...], u], axis=0)
    taps = cw_ref.shape[0]
    xc = cb_ref[...] + cw_ref[taps - 1:taps, :] * u
    for k in range(1, taps):
        xc = xc + cw_ref[taps - 1 - k:taps - k, :] * pltpu.roll(full, k, 0)[keep:]
    tail_ref[...] = u[tc - keep:]

    a, b = _lru_gates(xc, wa_ref, wx_ref, ba_ref, bx_ref, lam_ref)
    row = lax.broadcasted_iota(jnp.int32, (tc, 1), 0)
    d = 1
    while d < tc:
        valid = row >= d
        a_prev = jnp.where(valid, pltpu.roll(a, d, 0), 1.0)
        b_prev = jnp.where(valid, pltpu.roll(b, d, 0), 0.0)
        b = a * b_prev + b
        a = a * a_prev
        d *= 2
    hs = a * h_ref[...] + b
    h_last = hs[tc - 1:tc]
    h_ref[...] = h_last
    hl_ref[...] = h_last
    o_ref[...] = hs.astype(o_ref.dtype)


def _lru_prompt(z, col_block, conv_w, conv_b, wa_bd, wx_bd, ba, bx, lam, layer, batch, seq, width):
    tc = _tile(seq, 256, V7X_BF16_SUBLANES)
    nt = seq // tc
    vec = pl.BlockSpec((None, 1, width), lambda b, t: (layer, 0, 0))
    bd = pl.BlockSpec(wa_bd.shape, lambda b, t: (0, 0, 0))
    o, h_last = pl.pallas_call(
        functools.partial(_lru_prompt_kernel, tc=tc), grid=(batch, nt),
        in_specs=[pl.BlockSpec((tc, width), lambda b, t: (b * nt + t, col_block)),
                  pl.BlockSpec((None,) + conv_w.shape[1:], lambda b, t: (layer, 0, 0)),
                  vec, bd, bd, vec, vec, vec],
        out_specs=[pl.BlockSpec((tc, width), lambda b, t: (b * nt + t, 0)),
                   pl.BlockSpec((None, 1, width), lambda b, t: (b, 0, 0))],
        out_shape=[jax.ShapeDtypeStruct((batch * seq, width), BF16),
                   jax.ShapeDtypeStruct((batch, 1, width), F32)],
        scratch_shapes=[pltpu.VMEM((8, width), F32), pltpu.VMEM((1, width), F32)],
        compiler_params=_params(("arbitrary", "arbitrary")),
        name="lru_prompt")(z, conv_w, conv_b, wa_bd, wx_bd, ba, bx, lam)
    return o, h_last.reshape(batch, width)


def _lru_sample_kernel(u_ref, cs_ref, h0_ref, cw_ref, cb_ref, wa_ref, wx_ref, ba_ref, bx_ref, lam_ref,
                       o_ref, hn_ref):
    taps = cw_ref.shape[0]
    xc = cb_ref[...] + cw_ref[taps - 1:taps, :] * u_ref[...]
    for k in range(taps - 1):
        xc = xc + cw_ref[k:k + 1, :] * cs_ref[k]
    a, b = _lru_gates(xc, wa_ref, wx_ref, ba_ref, bx_ref, lam_ref)
    h = a * h0_ref[...] + b
    hn_ref[...] = h
    o_ref[...] = h.astype(o_ref.dtype)


def _lru_sample(z, row_block, col_block, conv_t, h0, conv_w, conv_b, wa_bd, wx_bd, ba, bx, lam,
                layer, nrows, width):
    vec = pl.BlockSpec((None, 1, width), lambda i: (layer, 0, 0))
    bd = pl.BlockSpec(wa_bd.shape, lambda i: (0, 0, 0))
    return pl.pallas_call(
        _lru_sample_kernel, grid=(1,),
        in_specs=[pl.BlockSpec((nrows, width), lambda i: (row_block, col_block)),
                  pl.BlockSpec(conv_t.shape, lambda i: (0, 0, 0)),
                  pl.BlockSpec((None, nrows, width), lambda i: (layer, 0, 0)),
                  pl.BlockSpec((None,) + conv_w.shape[1:], lambda i: (layer, 0, 0)),
                  vec, bd, bd, vec, vec, vec],
        out_specs=[pl.BlockSpec((nrows, width), lambda i: (0, 0))] * 2,
        out_shape=[jax.ShapeDtypeStruct((nrows, width), BF16), jax.ShapeDtypeStruct((nrows, width), F32)],
        compiler_params=_params(("arbitrary",)),
        name="lru_sample")(z, conv_t, h0, conv_w, conv_b, wa_bd, wx_bd, ba, bx, lam)


def _block_diag(w, chunk):
    heads, blk, _ = w.shape
    per = chunk // blk
    eye = jnp.eye(per, dtype=w.dtype)
    wr = w.reshape(heads // per, per, blk, blk)
    return jnp.einsum("cpij,pq->cpiqj", wr, eye).reshape(heads // per, chunk, chunk)


def _mla_prep_kernel(cq_ref, ckv_ref, kr_ref, cos_ref, sin_ref, gq_ref, gkv_ref,
                     cqn_ref, ckvn_ref, ckvb_ref, krn_ref, krb_ref):
    cqn_ref[...] = _rms(cq_ref[...], gq_ref[...]).astype(cqn_ref.dtype)
    ckv = _rms(ckv_ref[...], gkv_ref[...])
    ckvn_ref[...] = ckv
    ckvb_ref[...] = ckv.astype(ckvb_ref.dtype)
    x = kr_ref[...]
    lane = lax.broadcasted_iota(jnp.int32, x.shape, 1)
    half = QK_ROPE // 2
    partner = jnp.where(lane % QK_ROPE < half, pltpu.roll(x, V7X_LANES - half, 1), pltpu.roll(x, half, 1))
    kr = jnp.where(lane < QK_ROPE, x * cos_ref[...] + partner * sin_ref[...], 0.0)
    krn_ref[...] = kr[:, :QK_ROPE]
    krb_ref[...] = (kr + pltpu.roll(kr, QK_ROPE, 1)).astype(krb_ref.dtype)


def _mla_prep(z, z_kr, cos, sin, q_norm, kv_norm, layer, q_block, kv_block, q_lora, kv_lora):
    m = z.shape[0]
    tm = _tile(m, 512, V7X_BF16_SUBLANES)
    rows = lambda w, c: pl.BlockSpec((tm, w), lambda i: (i, c))
    return pl.pallas_call(
        _mla_prep_kernel, grid=(m // tm,),
        in_specs=[rows(q_lora, q_block), rows(kv_lora, kv_block), rows(V7X_LANES, 0),
                  rows(V7X_LANES, 0), rows(V7X_LANES, 0),
                  pl.BlockSpec((None, 1, q_lora), lambda i: (layer, 0, 0)),
                  pl.BlockSpec((None, 1, kv_lora), lambda i: (layer, 0, 0))],
        out_specs=[rows(q_lora, 0), rows(kv_lora, 0), rows(kv_lora, 0), rows(QK_ROPE, 0),
                   rows(V7X_LANES, 0)],
        out_shape=[jax.ShapeDtypeStruct((m, q_lora), BF16), jax.ShapeDtypeStruct((m, kv_lora), F32),
                   jax.ShapeDtypeStruct((m, kv_lora), BF16), jax.ShapeDtypeStruct((m, QK_ROPE), F32),
                   jax.ShapeDtypeStruct((m, V7X_LANES), BF16)],
        compiler_params=_params(("arbitrary",)), name="mla_prep")(z, z, z_kr, cos, sin, q_norm, kv_norm)


def _attn_prompt_kernel(qn_ref, qr_ref, kn_ref, kr_ref, v_ref, o_ref, *, tq, tk, scale):
    qi = pl.program_id(2)
    lane = lax.broadcasted_iota(jnp.int32, (tq, V7X_LANES), 1)
    qr_pair = qr_ref[...]
    qpos = qi * tq + lax.broadcasted_iota(jnp.int32, (tq, tk), 0)
    kidx = lax.broadcasted_iota(jnp.int32, (tq, tk), 1)
    nt = (((1,), (1,)), ((), ()))
    outs = []
    for hh in range(2):
        hs = slice(hh * QK_NOPE, (hh + 1) * QK_NOPE)
        qn = qn_ref[:, hs]
        in_head = (lane >= hh * QK_ROPE) & (lane < (hh + 1) * QK_ROPE)
        qr = jnp.where(in_head, qr_pair, jnp.zeros_like(qr_pair))

        def body(kb, carry):
            m, l, acc = carry
            off = pl.multiple_of(kb * tk, tk)
            kn = kn_ref[pl.ds(off, tk), hs]
            kr = kr_ref[pl.ds(off, tk), :]
            s = (lax.dot_general(qn, kn, nt, preferred_element_type=F32)
                 + lax.dot_general(qr, kr, nt, preferred_element_type=F32)) * scale
            s = jnp.where(kidx + off <= qpos, s, -jnp.inf)
            m_new = jnp.maximum(m, jnp.max(s, axis=1, keepdims=True))
            corr = jnp.exp(m - m_new)
            p = jnp.exp(s - m_new)
            l = l * corr + jnp.sum(p, axis=1, keepdims=True)
            acc = acc * corr + jnp.dot(p.astype(BF16), v_ref[pl.ds(off, tk), hs],
                                       preferred_element_type=F32)
            return m_new, l, acc

        init = (jnp.full((tq, 1), -jnp.inf, F32), jnp.zeros((tq, 1), F32), jnp.zeros((tq, V_HEAD), F32))
        nkb = (qi * tq + tq + tk - 1) // tk
        m, l, acc = lax.fori_loop(0, nkb, body, init)
        outs.append(acc / l)
    o_ref[...] = jnp.concatenate(outs, axis=1).astype(o_ref.dtype)


def _attn_prompt(qn, qr, kn, krb, v, batch, seq, heads, scale):
    tq = _tile(seq, 256, V7X_BF16_SUBLANES)
    tk = tq
    nq = seq // tq
    pair_n = 2 * QK_NOPE
    return pl.pallas_call(
        functools.partial(_attn_prompt_kernel, tq=tq, tk=tk, scale=scale), grid=(batch, heads // 2, nq),
        in_specs=[pl.BlockSpec((tq, pair_n), lambda b, h, q: (b * nq + q, h)),
                  pl.BlockSpec((tq, V7X_LANES), lambda b, h, q: (b * nq + q, h)),
                  pl.BlockSpec((seq, pair_n), lambda b, h, q: (b, h)),
                  pl.BlockSpec((seq, V7X_LANES), lambda b, h, q: (b, 0)),
                  pl.BlockSpec((seq, pair_n), lambda b, h, q: (b, h))],
        out_specs=pl.BlockSpec((tq, pair_n), lambda b, h, q: (b * nq + q, h)),
        out_shape=jax.ShapeDtypeStruct((batch * seq, heads * V_HEAD), BF16),
        compiler_params=_params(("arbitrary", "arbitrary", "arbitrary")),
        name="attn_prompt")(qn, qr, kn, krb, v)


def _head_map_kernel(x_ref, w_ref, o_ref, *, transpose_w):
    w = w_ref[...].astype(BF16)
    dims = (((1,), (1,)), ((), ())) if transpose_w else (((1,), (0,)), ((), ()))
    o_ref[...] = lax.dot_general(x_ref[...], w, dims, preferred_element_type=F32).astype(o_ref.dtype)


def _q_latent(qn, w_ukv, layer, row_block, nrows, heads, kv_lora):
    return pl.pallas_call(
        functools.partial(_head_map_kernel, transpose_w=True), grid=(heads,),
        in_specs=[pl.BlockSpec((nrows, QK_NOPE), lambda h: (row_block, h)),
                  pl.BlockSpec((None, kv_lora, QK_NOPE), lambda h: (layer, 0, 2 * h))],
        out_specs=pl.BlockSpec((nrows, kv_lora), lambda h: (0, h)),
        out_shape=jax.ShapeDtypeStruct((nrows, heads * kv_lora), BF16),
        compiler_params=_params(("arbitrary",)), name="q_latent")(qn, w_ukv)


def _v_up(o_lat, w_ukv, layer, nrows, heads, kv_lora):
    return pl.pallas_call(
        functools.partial(_head_map_kernel, transpose_w=False), grid=(heads,),
        in_specs=[pl.BlockSpec((nrows, kv_lora), lambda h: (0, h)),
                  pl.BlockSpec((None, kv_lora, V_HEAD), lambda h: (layer, 0, 2 * h + 1))],
        out_specs=pl.BlockSpec((nrows, V_HEAD), lambda h: (0, h)),
        out_shape=jax.ShapeDtypeStruct((nrows, heads * V_HEAD), BF16),
        compiler_params=_params(("arbitrary",)), name="v_up")(o_lat, w_ukv)


def _paged_kernel(pt_ref, ql_ref, qr_ref, cs_ref, ks_ref, *rest, group, scale):
    ck_refs = rest[:group]
    kp_refs = rest[group:2 * group]
    o_ref, m_ref, l_ref, acc_ref = rest[2 * group:]
    p = pl.program_id(1)
    nt = (((1,), (1,)), ((), ()))

    @pl.when(p == 0)
    def _():
        m_ref[...] = jnp.full_like(m_ref, -jnp.inf)
        l_ref[...] = jnp.zeros_like(l_ref)
        acc_ref[...] = jnp.zeros_like(acc_ref)

    ql = ql_ref[...]
    qr = qr_ref[...]
    m = m_ref[...]
    l = l_ref[...]
    acc = acc_ref[...]
    for j in range(group):
        ck = ck_refs[j][...].astype(BF16)
        kp = kp_refs[j][...].astype(BF16)
        s = (lax.dot_general(ql, ck, nt, preferred_element_type=F32)
             + lax.dot_general(qr, kp, nt, preferred_element_type=F32)) * scale
        m_new = jnp.maximum(m, jnp.max(s, axis=1, keepdims=True))
        corr = jnp.exp(m - m_new)
        pr = jnp.exp(s - m_new)
        l = l * corr + jnp.sum(pr, axis=1, keepdims=True)
        acc = acc * corr + jnp.dot(pr.astype(BF16), ck, preferred_element_type=F32)
        m = m_new
    m_ref[...] = m
    l_ref[...] = l
    acc_ref[...] = acc

    @pl.when(p == pl.num_programs(1) - 1)
    def _():
        cs = cs_ref[...].astype(F32)
        ks = ks_ref[...][:, :QK_ROPE].astype(F32)
        s = (jnp.sum(ql.astype(F32) * cs, axis=1, keepdims=True)
             + jnp.sum(qr.astype(F32) * ks, axis=1, keepdims=True)) * scale
        m_new = jnp.maximum(m, s)
        corr = jnp.exp(m - m_new)
        pr = jnp.exp(s - m_new)
        l_fin = l * corr + pr
        acc_fin = acc * corr + pr.astype(BF16).astype(F32) * cs
        o_ref[...] = (acc_fin / l_fin).astype(o_ref.dtype)


def _paged_attention(page_table, q_lat, q_rope, ckv_self, kr_self, cache_ckv, cache_krope, layer, scale):
    nseq, heads, kv_lora = q_lat.shape
    n_pages = page_table.shape[1]
    page = cache_ckv.shape[2]
    group = _tile(n_pages, 8, 1)
    seq_spec = lambda a: pl.BlockSpec((None,) + a.shape[1:], lambda b, p, pt: (b, 0, 0))

    def cache_spec(c, j):
        return pl.BlockSpec((None, None, page, c.shape[3]),
                            lambda b, p, pt: (layer, pt[b, p * group + j], 0, 0))

    grid_spec = pltpu.PrefetchScalarGridSpec(
        num_scalar_prefetch=1, grid=(nseq, n_pages // group),
        in_specs=[seq_spec(q_lat), seq_spec(q_rope), seq_spec(ckv_self), seq_spec(kr_self)]
                 + [cache_spec(cache_ckv, j) for j in range(group)]
                 + [cache_spec(cache_krope, j) for j in range(group)],
        out_specs=pl.BlockSpec((None, heads, kv_lora), lambda b, p, pt: (b, 0, 0)),
        scratch_shapes=[pltpu.VMEM((heads, 1), F32), pltpu.VMEM((heads, 1), F32),
                        pltpu.VMEM((heads, kv_lora), F32)])
    return pl.pallas_call(
        functools.partial(_paged_kernel, group=group, scale=scale), grid_spec=grid_spec,
        out_shape=jax.ShapeDtypeStruct((nseq, heads, kv_lora), BF16),
        compiler_params=_params(("arbitrary", "arbitrary")),
        name="paged_attention")(page_table, q_lat, q_rope, ckv_self, kr_self,
                                *([cache_ckv] * group), *([cache_krope] * group))


def _router_kernel(x_ref, w_ref, b_ref, o_ref):
    logits = jnp.dot(x_ref[...], w_ref[...].astype(BF16), preferred_element_type=F32) + b_ref[...]
    ne = logits.shape[1]
    idx = lax.broadcasted_iota(jnp.int32, logits.shape, 1)
    v1 = jnp.max(logits, axis=1, keepdims=True)
    i1 = jnp.min(jnp.where(logits == v1, idx, ne), axis=1, keepdims=True)
    rest = jnp.where(idx == i1, -jnp.inf, logits)
    v2 = jnp.max(rest, axis=1, keepdims=True)
    i2 = jnp.min(jnp.where(rest == v2, idx, ne), axis=1, keepdims=True)
    e2 = jnp.exp(v2 - v1)
    denom = 1.0 + e2
    o_ref[...] = jnp.where(idx == i1, 1.0 / denom, 0.0) + jnp.where(idx == i2, e2 / denom, 0.0)


def _router(x, w_router, b_router, j):
    m, d = x.shape
    ne = w_router.shape[-1]
    tm = _tile(m, 1024, V7X_BF16_SUBLANES)
    return pl.pallas_call(
        _router_kernel, grid=(m // tm,),
        in_specs=[pl.BlockSpec((tm, d), lambda i: (i, 0)),
                  pl.BlockSpec((None, d, ne), lambda i: (j, 0, 0)),
                  pl.BlockSpec((None, 1, ne), lambda i: (j, 0, 0))],
        out_specs=pl.BlockSpec((tm, ne), lambda i: (i, 0)),
        out_shape=jax.ShapeDtypeStruct((m, ne), F32),
        compiler_params=_params(("arbitrary",)), name="router_top2")(x, w_router, b_router)


def _rope_tables(positions):
    half = QK_ROPE // 2
    inv = ROPE_BASE ** (-jnp.arange(half, dtype=F32) / half)
    ang = positions.astype(F32)[:, None] * inv
    cos, sin = jnp.cos(ang), jnp.sin(ang)
    reps = V7X_LANES // QK_ROPE
    return (jnp.tile(jnp.concatenate([cos, cos], axis=1), (1, reps)),
            jnp.tile(jnp.concatenate([-sin, sin], axis=1), (1, reps)))


def kernel(x_prompt, x_sample, cache_ckv, cache_krope, state_pool, state_lru_conv, state_lru_h, page_table, norm_mix_pre, norm_mix_post, norm_ffn_pre, norm_ffn_post, w_in, pool_w, pool_scale, lru_conv_w, lru_conv_b, lru_wa, lru_ba, lru_wx, lru_bx, lru_lambda, mla_q_norm, mla_w_uq, mla_kv_norm, mla_w_ukv, w_branch_pool, w_branch_lru, w_branch_mla, w_out, ffn_w_gate, ffn_w_up, ffn_w_down, moe_w_router, moe_b_router, moe_w_gate, moe_w_up, moe_w_down):
    batch, seq, d = x_prompt.shape
    nseq, dec_seq, _ = x_sample.shape
    assert dec_seq == 1
    depth = w_in.shape[0]
    pool_width = pool_scale.shape[1]
    lru_width = lru_lambda.shape[1]
    q_lora = mla_q_norm.shape[1]
    kv_lora = mla_kv_norm.shape[1]
    qk_head = QK_NOPE + QK_ROPE
    heads = mla_w_uq.shape[2] // qk_head
    n_experts = moe_w_router.shape[2]
    past_len = page_table.shape[1] * cache_ckv.shape[2]
    scale = float(qk_head) ** -0.5
    mp = batch * seq
    m = mp + nseq
    o_lru_in = pool_width
    o_q = o_lru_in + lru_width
    o_kv = o_q + q_lora
    o_kr = o_kv + kv_lora
    o_gate = o_kr + QK_ROPE
    assert pool_width == lru_width == q_lora and o_kv % kv_lora == 0 and o_kr % V7X_LANES == 0
    assert mp % nseq == 0

    vec3 = lambda a: a.reshape(a.shape[0], 1, a.shape[1])
    pool_scale, lru_conv_b, lru_ba, lru_bx, lru_lambda, mla_q_norm, mla_kv_norm, moe_b_router = map(
        vec3, (pool_scale, lru_conv_b, lru_ba, lru_bx, lru_lambda, mla_q_norm, mla_kv_norm, moe_b_router))

    tm = _tile(m, 1040, V7X_BF16_SUBLANES)
    tm_small = _tile(m, 520, V7X_BF16_SUBLANES)
    tmp = _tile(mp, 1024, V7X_BF16_SUBLANES)
    tn = _tile(o_kr, 512, V7X_LANES)

    x = jnp.concatenate([x_prompt.reshape(mp, d), x_sample.reshape(nseq, d)], axis=0)
    positions = jnp.concatenate([jnp.tile(jnp.arange(seq), batch), jnp.full((nseq,), past_len)])
    cos, sin = _rope_tables(positions)

    outs_prompt = [[] for _ in range(5)]
    outs_sample = [[] for _ in range(5)]
    h = _norm(x, norm_mix_pre[0])
    for i in range(depth):
        z = _mm(h, w_in, prefix=(i,), ncols=o_kr, tn=tn, tm=tm, out_dtype=F32, name="in_proj")
        z_kr = _mm(h, w_in, prefix=(i,), col0=o_kr // V7X_LANES, ncols=V7X_LANES, tn=V7X_LANES, tm=tm,
                   out_dtype=F32, name="in_proj_kr")
        w_gate = w_in[i, :, o_gate:]
        gate_logits = _mm(h, w_gate, ncols=3 * d, tn=_tile(d, 512, V7X_LANES), tm=tm, out_dtype=F32,
                          name="in_proj_gates")

        op_p = _pool_prompt(z, pool_w, pool_scale, i, batch, seq, pool_width)
        op_s = _pool_sample(z, state_pool[i].swapaxes(0, 1), pool_w, pool_scale, i, mp // nseq, nseq,
                            pool_width)
        o_pool = jnp.concatenate([op_p, op_s], axis=0)
        u_pool_p = z[:mp, :pool_width].reshape(batch, seq, pool_width)
        outs_prompt[2].append(u_pool_p[:, seq - state_pool.shape[2]:])
        outs_sample[2].append(jnp.concatenate([state_pool[i, :, 1:], z[mp:, None, :pool_width]], axis=1))

        chunk = min(MXU_TILE, lru_width)
        wa_bd = _block_diag(lru_wa[i], chunk)
        wx_bd = _block_diag(lru_wx[i], chunk)
        ol_p, hl_p = _lru_prompt(z, o_lru_in // lru_width, lru_conv_w, lru_conv_b, wa_bd, wx_bd,
                                 lru_ba, lru_bx, lru_lambda, i, batch, seq, lru_width)
        ol_s, hl_s = _lru_sample(z, mp // nseq, o_lru_in // lru_width, state_lru_conv[i].swapaxes(0, 1),
                                 state_lru_h, lru_conv_w, lru_conv_b, wa_bd, wx_bd, lru_ba, lru_bx,
                                 lru_lambda, i, nseq, lru_width)
        o_lru = jnp.concatenate([ol_p, ol_s], axis=0)
        u_lru = z[:, o_lru_in:o_q]
        outs_prompt[3].append(u_lru[:mp].reshape(batch, seq, lru_width)[:, seq - state_lru_conv.shape[2]:])
        outs_sample[3].append(jnp.concatenate([state_lru_conv[i, :, 1:], u_lru[mp:, None]], axis=1))
        outs_prompt[4].append(hl_p)
        outs_sample[4].append(hl_s)

        cq, ckv, ckv_b, kr, kr_b = _mla_prep(z, z_kr, cos, sin, mla_q_norm, mla_kv_norm, i,
                                             o_q // q_lora, o_kv // kv_lora, q_lora, kv_lora)
        outs_prompt[0].append(ckv[:mp].reshape(batch, seq, kv_lora))
        outs_prompt[1].append(kr[:mp].reshape(batch, seq, QK_ROPE))
        outs_sample[0].append(ckv[mp:].reshape(nseq, 1, kv_lora))
        outs_sample[1].append(kr[mp:].reshape(nseq, 1, QK_ROPE))
        w_uq = mla_w_uq[i].reshape(q_lora, heads, qk_head)
        w_uq_n = w_uq[:, :, :QK_NOPE].reshape(q_lora, heads * QK_NOPE)
        w_uq_r = w_uq[:, :, QK_NOPE:].reshape(q_lora, heads * QK_ROPE)
        qn = _mm(cq, w_uq_n, ncols=heads * QK_NOPE, tn=_tile(heads * QK_NOPE, 1024, V7X_LANES), tm=tm,
                 out_dtype=BF16, name="q_nope")
        qr = _mm(cq, w_uq_r, ncols=heads * QK_ROPE, tn=_tile(heads * QK_ROPE, 1024, V7X_LANES), tm=tm,
                 out_dtype=BF16, epilogue="rope", extra=(cos, sin), name="q_rope")
        w_kv = mla_w_ukv[i].reshape(kv_lora, heads, QK_NOPE + V_HEAD)
        w_k = w_kv[:, :, :QK_NOPE].reshape(kv_lora, heads * QK_NOPE)
        w_v = w_kv[:, :, QK_NOPE:].reshape(kv_lora, heads * V_HEAD)
        kn = _mm(ckv_b, w_k, ncols=heads * QK_NOPE, tn=_tile(heads * QK_NOPE, 1024, V7X_LANES), tm=tmp,
                 rows=mp, out_dtype=BF16, name="k_nope")
        vv = _mm(ckv_b, w_v, ncols=heads * V_HEAD, tn=_tile(heads * V_HEAD, 1024, V7X_LANES), tm=tmp,
                 rows=mp, out_dtype=BF16, name="v_up_prompt")
        om_p = _attn_prompt(qn, qr, kn, kr_b, vv, batch, seq, heads, scale)

        q_lat = _q_latent(qn, mla_w_ukv, i, mp // nseq, nseq, heads, kv_lora)
        o_lat = _paged_attention(page_table, q_lat.reshape(nseq, heads, kv_lora),
                                 qr[mp:].reshape(nseq, heads, QK_ROPE),
                                 ckv_b[mp:].reshape(nseq, 1, kv_lora), kr_b[mp:].reshape(nseq, 1, V7X_LANES),
                                 cache_ckv, cache_krope, i, scale)
        om_s = _v_up(o_lat.reshape(nseq, heads * kv_lora), mla_w_ukv, i, nseq, heads, kv_lora)
        o_mla = jnp.concatenate([om_p, om_s], axis=0)

        merged = _merge(o_pool, o_lru, o_mla, w_branch_pool, w_branch_lru, w_branch_mla, gate_logits, i,
                        tn=_tile(d, 512, V7X_LANES), tm=tm_small)
        mix = _mm(merged, w_out, prefix=(i,), ncols=d, tn=_tile(d, 512, V7X_LANES), tm=tm, out_dtype=F32,
                  name="out_proj")
        x, h2 = _resid_norm(x, mix, norm_mix_post[i], norm_ffn_pre[i])

        j = i // 2
        if i % 2 == 0:
            d_ff = ffn_w_gate.shape[2]
            act = _swiglu_up(h2, ffn_w_gate, ffn_w_up, (j,), tn=_tile(d_ff, 256, V7X_LANES), tm=tm)
            f = _mm(act, ffn_w_down, prefix=(j,), ncols=d, tn=_tile(d, 256, V7X_LANES),
                    tm=_tile(m, 416, V7X_BF16_SUBLANES), out_dtype=F32, name="ffn_down")
        else:
            combine = _router(h2, moe_w_router, moe_b_router, j)
            d_ffe = moe_w_gate.shape[3]
            f = jnp.zeros((m, d), F32)
            for e in range(n_experts):
                act = _swiglu_up(h2, moe_w_gate, moe_w_up, (j, e), tn=_tile(d_ffe, 256, V7X_LANES), tm=tm)
                f = _mm(act, moe_w_down, prefix=(j, e), ncols=d, tn=_tile(d, 512, V7X_LANES), tm=tm,
                        out_dtype=F32, epilogue="scale_add", extra=(combine[:, e:e + 1], f),
                        name="moe_down")
        g_next = norm_mix_pre[i + 1] if i + 1 < depth else None
        x, h = _resid_norm(x, f, norm_ffn_post[i], g_next)

    y_prompt = x[:mp].reshape(batch, seq, d)
    y_sample = x[mp:].reshape(nseq, 1, d)
    st = lambda lst: jnp.stack(lst)
    return (y_prompt, y_sample, st(outs_prompt[0]), st(outs_prompt[1]), st(outs_prompt[2]),
            st(outs_prompt[3]), st(outs_prompt[4]), st(outs_sample[0]), st(outs_sample[1]),
            st(outs_sample[2]), st(outs_sample[3]), st(outs_sample[4]))
```

```python
import functools

import jax
import jax.numpy as jnp
from jax import lax
from jax.experimental import pallas as pl
from jax.experimental.pallas import tpu as pltpu

BF16 = jnp.bfloat16
F32 = jnp.float32

V7X_LANES = 128
V7X_BF16_SUBLANES = 16
V7X_VMEM_LIMIT_BYTES = 56 * 1024 * 1024

POOL_WINDOWS = (2, 4, 8, 16)
LRU_C = 8.0
QK_NOPE = 128
QK_ROPE = 64
V_HEAD = 128
ROPE_BASE = 10000.0
TOP_K = 2
EPS = 1e-6
MXU_TILE = 256


def _tile(n, target, mult):
    best = None
    for t in range(mult, min(n, target) + 1, mult):
        if n % t == 0:
            best = t
    return best if best is not None else n


def _params(sem):
    return pltpu.CompilerParams(dimension_semantics=sem, vmem_limit_bytes=V7X_VMEM_LIMIT_BYTES)


def _wspec(w, k, tn, prefix, col0):
    nlead = len(prefix)
    return pl.BlockSpec((None,) * nlead + (k, tn), lambda n, m: tuple(prefix) + (0, n + col0))


def _wspec_t(k, tn, prefix, row0):
    nlead = len(prefix)
    return pl.BlockSpec((pl.Element(1),) * nlead + (pl.Element(tn), pl.Element(k)),
                        lambda n, m: tuple(prefix) + (pl.multiple_of(row0 + n * tn, 8), 0))


def _rms(x, g):
    return x * lax.rsqrt(jnp.mean(x * x, axis=-1, keepdims=True) + EPS) * g


def _norm_kernel(x_ref, g_ref, h_ref):
    h_ref[...] = _rms(x_ref[...], g_ref[...]).astype(h_ref.dtype)


def _resid_norm_kernel(x_ref, f_ref, gp_ref, gn_ref, xo_ref, *h_refs):
    x = x_ref[...] + _rms(f_ref[...], gp_ref[...])
    xo_ref[...] = x
    h = _rms(x, gn_ref[...])
    for h_ref in h_refs:
        h_ref[...] = h.astype(h_ref.dtype)


def _resid_kernel(x_ref, f_ref, gp_ref, xo_ref):
    xo_ref[...] = x_ref[...] + _rms(f_ref[...], gp_ref[...])


def _norm(x, g):
    m, d = x.shape
    tm = _tile(m, 256, V7X_BF16_SUBLANES)
    row = pl.BlockSpec((tm, d), lambda i: (i, 0))
    vec = pl.BlockSpec((1, d), lambda i: (0, 0))
    return pl.pallas_call(
        _norm_kernel, grid=(m // tm,), in_specs=[row, vec], out_specs=row,
        out_shape=jax.ShapeDtypeStruct((m, d), BF16), compiler_params=_params(("arbitrary",)),
        name="rms_norm")(x, g.reshape(1, d))


def _resid_norm(x, f, g_post, g_next, also_f32=False):
    m, d = x.shape
    tm = _tile(m, 128 if also_f32 else 256, V7X_BF16_SUBLANES)
    row = pl.BlockSpec((tm, d), lambda i: (i, 0))
    vec = pl.BlockSpec((1, d), lambda i: (0, 0))
    if g_next is None:
        return pl.pallas_call(
            _resid_kernel, grid=(m // tm,), in_specs=[row, row, vec], out_specs=row,
            out_shape=jax.ShapeDtypeStruct((m, d), F32), compiler_params=_params(("arbitrary",)),
            name="resid")(x, f, g_post.reshape(1, d)), None
    h_dtypes = (BF16, F32) if also_f32 else (BF16,)
    return pl.pallas_call(
        _resid_norm_kernel, grid=(m // tm,), in_specs=[row, row, vec, vec],
        out_specs=[row] * (1 + len(h_dtypes)),
        out_shape=[jax.ShapeDtypeStruct((m, d), F32)] + [jax.ShapeDtypeStruct((m, d), t) for t in h_dtypes],
        compiler_params=_params(("arbitrary",)),
        name="resid_norm")(x, f, g_post.reshape(1, d), g_next.reshape(1, d))


def _rope_tile(acc, cos_ref, sin_ref):
    tn = acc.shape[1]
    reps = tn // V7X_LANES
    cos = jnp.tile(cos_ref[...], (1, reps))
    sin = jnp.tile(sin_ref[...], (1, reps))
    lane = lax.broadcasted_iota(jnp.int32, acc.shape, 1) % QK_ROPE
    half = QK_ROPE // 2
    partner = jnp.where(lane < half, pltpu.roll(acc, tn - half, 1), pltpu.roll(acc, half, 1))
    return acc * cos + partner * sin


_NT = (((1,), (1,)), ((), ()))


def _mm_kernel(*refs, epilogue, w_rows):
    if epilogue == "rope":
        x_ref, w_ref, cos_ref, sin_ref, o_ref, wb_ref = refs
    else:
        x_ref, w_ref, o_ref, wb_ref = refs

    @pl.when(pl.program_id(1) == 0)
    def _():
        wb_ref[...] = w_ref[(0,) * (len(w_ref.shape) - 2)].astype(BF16)

    if w_rows:
        acc = lax.dot_general(x_ref[...], wb_ref[...], _NT, preferred_element_type=F32)
    else:
        acc = jnp.dot(x_ref[...], wb_ref[...], preferred_element_type=F32)
    if epilogue == "rope":
        acc = _rope_tile(acc, cos_ref, sin_ref)
    o_ref[...] = acc.astype(o_ref.dtype)


def _mm(x, w, *, prefix=(), col0=0, ncols, tn, tm, out_dtype, rows=None, row0=0,
        epilogue="none", extra=(), name="mm", w_rows=False):
    m, k = x.shape
    rows = m if rows is None else rows
    assert rows % tm == 0 and ncols % tn == 0
    grid = (ncols // tn, rows // tm)
    wspec = _wspec_t(k, tn, prefix, col0) if w_rows else _wspec(w, k, tn, prefix, col0)
    in_specs = [pl.BlockSpec((tm, k), lambda n, i: (i + row0, 0)), wspec]
    if epilogue == "rope":
        in_specs += [pl.BlockSpec((tm, V7X_LANES), lambda n, i: (i + row0, 0))] * 2
    return pl.pallas_call(
        functools.partial(_mm_kernel, epilogue=epilogue, w_rows=w_rows), grid=grid, in_specs=in_specs,
        out_specs=pl.BlockSpec((tm, tn), lambda n, i: (i, n)),
        out_shape=jax.ShapeDtypeStruct((rows, ncols), out_dtype),
        scratch_shapes=[pltpu.VMEM((tn, k) if w_rows else (k, tn), BF16)],
        compiler_params=_params(("arbitrary", "arbitrary")), name=name)(x, w, *extra)


def _swiglu_kernel(x_ref, wg_ref, wu_ref, o_ref, wgb_ref, wub_ref):
    @pl.when(pl.program_id(1) == 0)
    def _():
        wgb_ref[...] = wg_ref[...].astype(BF16)
        wub_ref[...] = wu_ref[...].astype(BF16)

    x = x_ref[...]
    a = jnp.dot(x, wgb_ref[...], preferred_element_type=F32)
    b = jnp.dot(x, wub_ref[...], preferred_element_type=F32)
    o_ref[...] = (a * jax.nn.sigmoid(a) * b).astype(o_ref.dtype)


def _swiglu_up(x, wg, wu, prefix, *, tn, tm):
    m, k = x.shape
    n = wg.shape[-1]
    return pl.pallas_call(
        _swiglu_kernel, grid=(n // tn, m // tm),
        in_specs=[pl.BlockSpec((tm, k), lambda j, i: (i, 0)),
                  _wspec(wg, k, tn, prefix, 0), _wspec(wu, k, tn, prefix, 0)],
        out_specs=pl.BlockSpec((tm, tn), lambda j, i: (i, j)),
        out_shape=jax.ShapeDtypeStruct((m, n), BF16),
        scratch_shapes=[pltpu.VMEM((k, tn), BF16), pltpu.VMEM((k, tn), BF16)],
        compiler_params=_params(("arbitrary", "arbitrary")), name="swiglu_up")(x, wg, wu)


def _merge_kernel(op_ref, ol_ref, om_ref, wp_ref, wl_ref, wm_ref, gp_ref, gl_ref, gm_ref,
                  o_ref, wpb_ref, wlb_ref, wmb_ref):
    @pl.when(pl.program_id(1) == 0)
    def _():
        wpb_ref[...] = wp_ref[...].astype(BF16)
        wlb_ref[...] = wl_ref[...].astype(BF16)
        wmb_ref[...] = wm_ref[...].astype(BF16)

    acc = jax.nn.sigmoid(gp_ref[...]) * jnp.dot(op_ref[...], wpb_ref[...], preferred_element_type=F32)
    acc += jax.nn.sigmoid(gl_ref[...]) * jnp.dot(ol_ref[...], wlb_ref[...], preferred_element_type=F32)
    acc += jax.nn.sigmoid(gm_ref[...]) * jnp.dot(om_ref[...], wmb_ref[...], preferred_element_type=F32)
    o_ref[...] = acc.astype(o_ref.dtype)


def _merge(o_pool, o_lru, o_mla, wbp, wbl, wbm, gate_logits, layer, *, tn, tm):
    m = o_pool.shape[0]
    d = wbp.shape[-1]
    nb = d // tn
    lhs = lambda a: pl.BlockSpec((tm, a.shape[1]), lambda n, i: (i, 0))
    wsp = lambda w: _wspec(w, w.shape[-2], tn, (layer,), 0)
    gate = lambda g: pl.BlockSpec((tm, tn), lambda n, i: (i, n + g * nb))
    return pl.pallas_call(
        _merge_kernel, grid=(nb, m // tm),
        in_specs=[lhs(o_pool), lhs(o_lru), lhs(o_mla), wsp(wbp), wsp(wbl), wsp(wbm),
                  gate(0), gate(1), gate(2)],
        out_specs=pl.BlockSpec((tm, tn), lambda n, i: (i, n)),
        out_shape=jax.ShapeDtypeStruct((m, d), BF16),
        scratch_shapes=[pltpu.VMEM((wbp.shape[-2], tn), BF16), pltpu.VMEM((wbl.shape[-2], tn), BF16),
                        pltpu.VMEM((wbm.shape[-2], tn), BF16)],
        compiler_params=_params(("arbitrary", "arbitrary")),
        name="gated_merge")(o_pool, o_lru, o_mla, wbp, wbl, wbm, gate_logits, gate_logits, gate_logits)


def _pool_project(sums, u, cnts, pw_ref, ps_ref, group):
    outs = []
    for g in range(len(POOL_WINDOWS)):
        sl = slice(g * group, (g + 1) * group)
        pooled = sums[g][:, sl] / cnts[g] - u[:, sl]
        outs.append(jnp.dot(pooled.astype(BF16), pw_ref[g].astype(BF16), preferred_element_type=F32))
    return jnp.concatenate(outs, axis=1) * ps_ref[...]


def _pool_prompt_kernel(u_ref, halo_ref, pw_ref, ps_ref, o_ref, *, tt, group):
    t = pl.program_id(1)
    halo_rows = halo_ref.shape[0]
    halo = jnp.where(t == 0, 0.0, halo_ref[...])
    u = u_ref[...]
    full = jnp.concatenate([halo, u], axis=0)
    sums = []
    s = full
    span = 1
    for win in POOL_WINDOWS:
        while span < win:
            s = s + pltpu.roll(s, span, 0)
            span *= 2
        sums.append(s[halo_rows:])
    pos = t * tt + lax.broadcasted_iota(jnp.int32, (tt, 1), 0)
    cnts = [jnp.minimum(pos + 1, win).astype(F32) for win in POOL_WINDOWS]
    o_ref[...] = _pool_project(sums, u, cnts, pw_ref, ps_ref, group).astype(o_ref.dtype)


def _pool_prompt(z, pool_w, pool_scale, layer, batch, seq, width):
    tt = _tile(seq, 256, V7X_BF16_SUBLANES)
    halo = 16
    nt = seq // tt
    group = width // len(POOL_WINDOWS)
    return pl.pallas_call(
        functools.partial(_pool_prompt_kernel, tt=tt, group=group), grid=(batch, nt),
        in_specs=[pl.BlockSpec((tt, width), lambda b, t: (b * nt + t, 0)),
                  pl.BlockSpec((halo, width),
                               lambda b, t: (jnp.maximum((b * nt + t) * (tt // halo) - 1, 0), 0)),
                  pl.BlockSpec((None,) + pool_w.shape[1:], lambda b, t: (layer, 0, 0, 0)),
                  pl.BlockSpec((None, 1, width), lambda b, t: (layer, 0, 0))],
        out_specs=pl.BlockSpec((tt, width), lambda b, t: (b * nt + t, 0)),
        out_shape=jax.ShapeDtypeStruct((batch * seq, width), BF16),
        compiler_params=_params(("arbitrary", "arbitrary")), name="pool_prompt")(z, z, pool_w, pool_scale)


def _pool_sample_kernel(u_ref, st_ref, pw_ref, ps_ref, o_ref, *, group):
    u = u_ref[...]
    nbuf = st_ref.shape[0]
    sums = []
    s = u
    taken = 0
    for win in POOL_WINDOWS:
        while taken < win - 1:
            taken += 1
            s = s + st_ref[nbuf - taken]
        sums.append(s)
    cnts = [float(win) for win in POOL_WINDOWS]
    o_ref[...] = _pool_project(sums, u, cnts, pw_ref, ps_ref, group).astype(o_ref.dtype)


def _pool_sample(z, state_t, pool_w, pool_scale, layer, row_block, nrows, width):
    group = width // len(POOL_WINDOWS)
    return pl.pallas_call(
        functools.partial(_pool_sample_kernel, group=group), grid=(1,),
        in_specs=[pl.BlockSpec((nrows, width), lambda i: (row_block, 0)),
                  pl.BlockSpec(state_t.shape, lambda i: (0, 0, 0)),
                  pl.BlockSpec((None,) + pool_w.shape[1:], lambda i: (layer, 0, 0, 0)),
                  pl.BlockSpec((None, 1, width), lambda i: (layer, 0, 0))],
        out_specs=pl.BlockSpec((nrows, width), lambda i: (0, 0)),
        out_shape=jax.ShapeDtypeStruct((nrows, width), BF16),
        compiler_params=_params(("arbitrary",)), name="pool_sample")(z, state_t, pool_w, pool_scale)


def _lru_gates(xc, wa_ref, wx_ref, ba_ref, bx_ref, lam_ref):
    nchunk, cw, _ = wa_ref.shape
    xb = xc.astype(BF16)
    ra, rx = [], []
    for c in range(nchunk):
        xs = xb[:, c * cw:(c + 1) * cw]
        ra.append(jnp.dot(xs, wa_ref[c].astype(BF16), preferred_element_type=F32))
        rx.append(jnp.dot(xs, wx_ref[c].astype(BF16), preferred_element_type=F32))
    r = jax.nn.sigmoid(jnp.concatenate(ra, axis=1) + ba_ref[...])
    gi = jax.nn.sigmoid(jnp.concatenate(rx, axis=1) + bx_ref[...])
    nl = -lam_ref[...]
    softplus = jnp.maximum(nl, 0.0) + jnp.log1p(jnp.exp(-jnp.abs(nl)))
    log_a = -LRU_C * r * softplus
    a = jnp.exp(log_a)
    b = jnp.sqrt(jnp.tanh(-log_a) * (a * a + 1.0)) * (gi * xc)
    return a, b


def _lru_prompt_kernel(u_ref, cw_ref, cb_ref, wa_ref, wx_ref, ba_ref, bx_ref, lam_ref,
                       o_ref, hl_ref, tail_ref, h_ref, *, tc):
    t = pl.program_id(1)
    keep = tail_ref.shape[0]

    @pl.when(t == 0)
    def _():
        tail_ref[...] = jnp.zeros_like(tail_ref)
        h_ref[...] = jnp.zeros_like(h_ref)

    u = u_ref[...]
    full = jnp.concatenate([tail_ref[...], u], axis=0)
    taps = cw_ref.shape[0]
    xc = cb_ref[...] + cw_ref[taps - 1:taps, :] * u
    for k in range(1, taps):
        xc = xc + cw_ref[taps - 1 - k:taps - k, :] * pltpu.roll(full, k, 0)[keep:]
    tail_ref[...] = u[tc - keep:]

    a, b = _lru_gates(xc, wa_ref, wx_ref, ba_ref, bx_ref, lam_ref)
    row = lax.broadcasted_iota(jnp.int32, (tc, 1), 0)
    d = 1
    while d < tc:
        valid = row >= d
        a_prev = jnp.where(valid, pltpu.roll(a, d, 0), 1.0)
        b_prev = jnp.where(valid, pltpu.roll(b, d, 0), 0.0)
        b = a * b_prev + b
        a = a * a_prev
        d *= 2
    hs = a * h_ref[...] + b
    h_last = hs[tc - 1:tc]
    h_ref[...] = h_last
    hl_ref[...] = h_last
    o_ref[...] = hs.astype(o_ref.dtype)


def _lru_prompt(z, col_block, conv_w, conv_b, wa_bd, wx_bd, ba, bx, lam, layer, batch, seq, width):
    tc = _tile(seq, 256, V7X_BF16_SUBLANES)
    nt = seq // tc
    vec = pl.BlockSpec((None, 1, width), lambda b, t: (layer, 0, 0))
    bd = pl.BlockSpec(wa_bd.shape, lambda b, t: (0, 0, 0))
    o, h_last = pl.pallas_call(
        functools.partial(_lru_prompt_kernel, tc=tc), grid=(batch, nt),
        in_specs=[pl.BlockSpec((tc, width), lambda b, t: (b * nt + t, col_block)),
                  pl.BlockSpec((None,) + conv_w.shape[1:], lambda b, t: (layer, 0, 0)),
                  vec, bd, bd, vec, vec, vec],
        out_specs=[pl.BlockSpec((tc, width), lambda b, t: (b * nt + t, 0)),
                   pl.BlockSpec((None, 1, width), lambda b, t: (b, 0, 0))],
        out_shape=[jax.ShapeDtypeStruct((batch * seq, width), BF16),
                   jax.ShapeDtypeStruct((batch, 1, width), F32)],
        scratch_shapes=[pltpu.VMEM((8, width), F32), pltpu.VMEM((1, width), F32)],
        compiler_params=_params(("arbitrary", "arbitrary")),
        name="lru_prompt")(z, conv_w, conv_b, wa_bd, wx_bd, ba, bx, lam)
    return o, h_last.reshape(batch, width)


def _lru_sample_kernel(u_ref, cs_ref, h0_ref, cw_ref, cb_ref, wa_ref, wx_ref, ba_ref, bx_ref, lam_ref,
                       o_ref, hn_ref):
    taps = cw_ref.shape[0]
    xc = cb_ref[...] + cw_ref[taps - 1:taps, :] * u_ref[...]
    for k in range(taps - 1):
        xc = xc + cw_ref[k:k + 1, :] * cs_ref[k]
    a, b = _lru_gates(xc, wa_ref, wx_ref, ba_ref, bx_ref, lam_ref)
    h = a * h0_ref[...] + b
    hn_ref[...] = h
    o_ref[...] = h.astype(o_ref.dtype)


def _lru_sample(z, row_block, col_block, conv_t, h0, conv_w, conv_b, wa_bd, wx_bd, ba, bx, lam,
                layer, nrows, width):
    vec = pl.BlockSpec((None, 1, width), lambda i: (layer, 0, 0))
    bd = pl.BlockSpec(wa_bd.shape, lambda i: (0, 0, 0))
    return pl.pallas_call(
        _lru_sample_kernel, grid=(1,),
        in_specs=[pl.BlockSpec((nrows, width), lambda i: (row_block, col_block)),
                  pl.BlockSpec(conv_t.shape, lambda i: (0, 0, 0)),
                  pl.BlockSpec((None, nrows, width), lambda i: (layer, 0, 0)),
                  pl.BlockSpec((None,) + conv_w.shape[1:], lambda i: (layer, 0, 0)),
                  vec, bd, bd, vec, vec, vec],
        out_specs=[pl.BlockSpec((nrows, width), lambda i: (0, 0))] * 2,
        out_shape=[jax.ShapeDtypeStruct((nrows, width), BF16), jax.ShapeDtypeStruct((nrows, width), F32)],
        compiler_params=_params(("arbitrary",)),
        name="lru_sample")(z, conv_t, h0, conv_w, conv_b, wa_bd, wx_bd, ba, bx, lam)


def _block_diag(w, chunk):
    heads, blk, _ = w.shape
    per = chunk // blk
    eye = jnp.eye(per, dtype=w.dtype)
    wr = w.reshape(heads // per, per, blk, blk)
    return jnp.einsum("cpij,pq->cpiqj", wr, eye).reshape(heads // per, chunk, chunk)


def _mla_prep_kernel(cq_ref, ckv_ref, kr_ref, cos_ref, sin_ref, gq_ref, gkv_ref,
                     cqn_ref, ckvn_ref, ckvb_ref, krn_ref, krb_ref):
    cqn_ref[...] = _rms(cq_ref[...], gq_ref[...]).astype(cqn_ref.dtype)
    ckv = _rms(ckv_ref[...], gkv_ref[...])
    ckvn_ref[...] = ckv
    ckvb_ref[...] = ckv.astype(ckvb_ref.dtype)
    x = kr_ref[...]
    lane = lax.broadcasted_iota(jnp.int32, x.shape, 1)
    half = QK_ROPE // 2
    partner = jnp.where(lane % QK_ROPE < half, pltpu.roll(x, V7X_LANES - half, 1), pltpu.roll(x, half, 1))
    kr = jnp.where(lane < QK_ROPE, x * cos_ref[...] + partner * sin_ref[...], 0.0)
    krn_ref[...] = kr[:, :QK_ROPE]
    krb_ref[...] = (kr + pltpu.roll(kr, QK_ROPE, 1)).astype(krb_ref.dtype)


def _mla_prep(z, z_kr, cos, sin, q_norm, kv_norm, layer, q_block, kv_block, q_lora, kv_lora):
    m = z.shape[0]
    tm = _tile(m, 512, V7X_BF16_SUBLANES)
    rows = lambda w, c: pl.BlockSpec((tm, w), lambda i: (i, c))
    return pl.pallas_call(
        _mla_prep_kernel, grid=(m // tm,),
        in_specs=[rows(q_lora, q_block), rows(kv_lora, kv_block), rows(V7X_LANES, 0),
                  rows(V7X_LANES, 0), rows(V7X_LANES, 0),
                  pl.BlockSpec((None, 1, q_lora), lambda i: (layer, 0, 0)),
                  pl.BlockSpec((None, 1, kv_lora), lambda i: (layer, 0, 0))],
        out_specs=[rows(q_lora, 0), rows(kv_lora, 0), rows(kv_lora, 0), rows(QK_ROPE, 0),
                   rows(V7X_LANES, 0)],
        out_shape=[jax.ShapeDtypeStruct((m, q_lora), BF16), jax.ShapeDtypeStruct((m, kv_lora), F32),
                   jax.ShapeDtypeStruct((m, kv_lora), BF16), jax.ShapeDtypeStruct((m, QK_ROPE), F32),
                   jax.ShapeDtypeStruct((m, V7X_LANES), BF16)],
        compiler_params=_params(("arbitrary",)), name="mla_prep")(z, z, z_kr, cos, sin, q_norm, kv_norm)


def _attn_prompt_kernel(qn_ref, qr_ref, kn_ref, kr_ref, v_ref, o_ref, *, tq, tk, scale):
    qi = pl.program_id(2)
    lane = lax.broadcasted_iota(jnp.int32, (tq, V7X_LANES), 1)
    qr_pair = qr_ref[...]
    qpos = qi * tq + lax.broadcasted_iota(jnp.int32, (tq, tk), 0)
    kidx = lax.broadcasted_iota(jnp.int32, (tq, tk), 1)
    nt = (((1,), (1,)), ((), ()))
    outs = []
    for hh in range(2):
        hs = slice(hh * QK_NOPE, (hh + 1) * QK_NOPE)
        qn = qn_ref[:, hs]
        in_head = (lane >= hh * QK_ROPE) & (lane < (hh + 1) * QK_ROPE)
        qr = jnp.where(in_head, qr_pair, jnp.zeros_like(qr_pair))

        def body(kb, carry):
            m, l, acc = carry
            off = pl.multiple_of(kb * tk, tk)
            kn = kn_ref[pl.ds(off, tk), hs]
            kr = kr_ref[pl.ds(off, tk), :]
            s = (lax.dot_general(qn, kn, nt, preferred_element_type=F32)
                 + lax.dot_general(qr, kr, nt, preferred_element_type=F32)) * scale
            s = jnp.where(kidx + off <= qpos, s, -jnp.inf)
            m_new = jnp.maximum(m, jnp.max(s, axis=1, keepdims=True))
            corr = jnp.exp(m - m_new)
            p = jnp.exp(s - m_new)
            l = l * corr + jnp.sum(p, axis=1, keepdims=True)
            acc = acc * corr + jnp.dot(p.astype(BF16), v_ref[pl.ds(off, tk), hs],
                                       preferred_element_type=F32)
            return m_new, l, acc

        init = (jnp.full((tq, 1), -jnp.inf, F32), jnp.zeros((tq, 1), F32), jnp.zeros((tq, V_HEAD), F32))
        nkb = (qi * tq + tq + tk - 1) // tk
        m, l, acc = lax.fori_loop(0, nkb, body, init)
        outs.append(acc / l)
    o_ref[...] = jnp.concatenate(outs, axis=1).astype(o_ref.dtype)


def _attn_prompt(qn, qr, kn, krb, v, batch, seq, heads, scale):
    tq = _tile(seq, 256, V7X_BF16_SUBLANES)
    tk = tq
    nq = seq // tq
    pair_n = 2 * QK_NOPE
    return pl.pallas_call(
        functools.partial(_attn_prompt_kernel, tq=tq, tk=tk, scale=scale), grid=(batch, heads // 2, nq),
        in_specs=[pl.BlockSpec((tq, pair_n), lambda b, h, q: (b * nq + q, h)),
                  pl.BlockSpec((tq, V7X_LANES), lambda b, h, q: (b * nq + q, h)),
                  pl.BlockSpec((seq, pair_n), lambda b, h, q: (b, h)),
                  pl.BlockSpec((seq, V7X_LANES), lambda b, h, q: (b, 0)),
                  pl.BlockSpec((seq, pair_n), lambda b, h, q: (b, h))],
        out_specs=pl.BlockSpec((tq, pair_n), lambda b, h, q: (b * nq + q, h)),
        out_shape=jax.ShapeDtypeStruct((batch * seq, heads * V_HEAD), BF16),
        compiler_params=_params(("arbitrary", "arbitrary", "arbitrary")),
        name="attn_prompt")(qn, qr, kn, krb, v)


def _head_map_kernel(x_ref, w_ref, o_ref, *, transpose_w):
    w = w_ref[...].astype(BF16)
    dims = (((1,), (1,)), ((), ())) if transpose_w else (((1,), (0,)), ((), ()))
    o_ref[...] = lax.dot_general(x_ref[...], w, dims, preferred_element_type=F32).astype(o_ref.dtype)


def _q_latent(qn, w_ukv, layer, row_block, nrows, heads, kv_lora):
    return pl.pallas_call(
        functools.partial(_head_map_kernel, transpose_w=True), grid=(heads,),
        in_specs=[pl.BlockSpec((nrows, QK_NOPE), lambda h: (row_block, h)),
                  pl.BlockSpec((None, kv_lora, QK_NOPE), lambda h: (layer, 0, 2 * h))],
        out_specs=pl.BlockSpec((nrows, kv_lora), lambda h: (0, h)),
        out_shape=jax.ShapeDtypeStruct((nrows, heads * kv_lora), BF16),
        compiler_params=_params(("arbitrary",)), name="q_latent")(qn, w_ukv)


def _v_up(o_lat, w_ukv, layer, nrows, heads, kv_lora):
    return pl.pallas_call(
        functools.partial(_head_map_kernel, transpose_w=False), grid=(heads,),
        in_specs=[pl.BlockSpec((nrows, kv_lora), lambda h: (0, h)),
                  pl.BlockSpec((None, kv_lora, V_HEAD), lambda h: (layer, 0, 2 * h + 1))],
        out_specs=pl.BlockSpec((nrows, V_HEAD), lambda h: (0, h)),
        out_shape=jax.ShapeDtypeStruct((nrows, heads * V_HEAD), BF16),
        compiler_params=_params(("arbitrary",)), name="v_up")(o_lat, w_ukv)


def _paged_kernel(pt_ref, ql_ref, qr_ref, cs_ref, ks_ref, ck_hbm, kp_hbm, o_ref, ck_buf, kp_buf, sem,
                  *, layer, group, n_groups, scale):
    b = pl.program_id(0)
    nb = pl.num_programs(0)
    page = ck_buf.shape[2]

    def group_copies(seq, g, slot):
        copies = []
        for j in range(group):
            phys = pt_ref[seq, g * group + j]
            copies.append(pltpu.make_async_copy(ck_hbm.at[layer, phys], ck_buf.at[slot, j], sem.at[slot, 0]))
            copies.append(pltpu.make_async_copy(kp_hbm.at[layer, phys], kp_buf.at[slot, j], sem.at[slot, 1]))
        return copies

    @pl.when(b == 0)
    def _():
        for c in group_copies(0, 0, 0):
            c.start()

    ql = ql_ref[...]
    qr = qr_ref[...]

    def body(g, carry):
        m, l, acc = carry
        slot = (b * n_groups + g) % 2

        @pl.when(g + 1 < n_groups)
        def _():
            for c in group_copies(b, g + 1, 1 - slot):
                c.start()

        @pl.when((g + 1 == n_groups) & (b + 1 < nb))
        def _():
            for c in group_copies(b + 1, 0, 1 - slot):
                c.start()

        for c in group_copies(b, g, slot):
            c.wait()

        ck = ck_buf[slot].reshape(group * page, ck_buf.shape[3]).astype(BF16)
        s_rope = jnp.concatenate(
            [jnp.dot(qr, kp_buf[slot, j].astype(BF16), preferred_element_type=F32) for j in range(group)],
            axis=1)
        s = (lax.dot_general(ql, ck, _NT, preferred_element_type=F32) + s_rope) * scale
        m_new = jnp.maximum(m, jnp.max(s, axis=1, keepdims=True))
        corr = jnp.exp(m - m_new)
        pr = jnp.exp(s - m_new)
        l = l * corr + jnp.sum(pr, axis=1, keepdims=True)
        acc = acc * corr + jnp.dot(pr.astype(BF16), ck, preferred_element_type=F32)
        return m_new, l, acc

    heads = ql.shape[0]
    init = (jnp.full((heads, 1), -jnp.inf, F32), jnp.zeros((heads, 1), F32),
            jnp.zeros((heads, ck_buf.shape[3]), F32))
    m, l, acc = lax.fori_loop(0, n_groups, body, init)

    cs = cs_ref[...].astype(F32)
    ks = ks_ref[...][:, :QK_ROPE].astype(F32)
    s = (jnp.sum(ql.astype(F32) * cs, axis=1, keepdims=True)
         + jnp.sum(qr.astype(F32) * ks, axis=1, keepdims=True)) * scale
    m_new = jnp.maximum(m, s)
    corr = jnp.exp(m - m_new)
    pr = jnp.exp(s - m_new)
    l_fin = l * corr + pr
    acc_fin = acc * corr + pr.astype(BF16).astype(F32) * cs
    o_ref[...] = (acc_fin / l_fin).astype(o_ref.dtype)


def _paged_attention(page_table, q_lat, q_rope, ckv_self, kr_self, cache_ckv, cache_krope_t, layer, scale):
    nseq, heads, kv_lora = q_lat.shape
    n_pages = page_table.shape[1]
    page = cache_ckv.shape[2]
    group = _tile(n_pages, 16, 1)
    n_groups = n_pages // group
    seq_spec = lambda a: pl.BlockSpec((None,) + a.shape[1:], lambda b, pt: (b, 0, 0))
    grid_spec = pltpu.PrefetchScalarGridSpec(
        num_scalar_prefetch=1, grid=(nseq,),
        in_specs=[seq_spec(q_lat), seq_spec(q_rope), seq_spec(ckv_self), seq_spec(kr_self),
                  pl.BlockSpec(memory_space=pl.ANY), pl.BlockSpec(memory_space=pl.ANY)],
        out_specs=pl.BlockSpec((None, heads, kv_lora), lambda b, pt: (b, 0, 0)),
        scratch_shapes=[pltpu.VMEM((2, group, page, kv_lora), F32),
                        pltpu.VMEM((2, group, QK_ROPE, page), F32),
                        pltpu.SemaphoreType.DMA((2, 2))])
    return pl.pallas_call(
        functools.partial(_paged_kernel, layer=layer, group=group, n_groups=n_groups, scale=scale),
        grid_spec=grid_spec,
        out_shape=jax.ShapeDtypeStruct((nseq, heads, kv_lora), BF16),
        compiler_params=_params(("arbitrary",)),
        name="paged_attention")(page_table, q_lat, q_rope, ckv_self, kr_self, cache_ckv, cache_krope_t)


def _router_kernel(x_ref, w_ref, b_ref, idx_ref, wgt_ref, rank_ref, cnt_ref, carry_ref):
    @pl.when(pl.program_id(0) == 0)
    def _():
        carry_ref[...] = jnp.zeros_like(carry_ref)

    logits = jnp.dot(x_ref[...], w_ref[...].astype(BF16), preferred_element_type=F32) + b_ref[...]
    tm, ne = logits.shape
    idx = lax.broadcasted_iota(jnp.int32, logits.shape, 1)
    v1 = jnp.max(logits, axis=1, keepdims=True)
    i1 = jnp.min(jnp.where(logits == v1, idx, ne), axis=1, keepdims=True)
    rest = jnp.where(idx == i1, -jnp.inf, logits)
    v2 = jnp.max(rest, axis=1, keepdims=True)
    i2 = jnp.min(jnp.where(rest == v2, idx, ne), axis=1, keepdims=True)
    e2 = jnp.exp(v2 - v1)
    denom = 1.0 + e2

    sel = ((idx == i1) | (idx == i2)).astype(F32)
    earlier = (lax.broadcasted_iota(jnp.int32, (tm, tm), 1)
               < lax.broadcasted_iota(jnp.int32, (tm, tm), 0)).astype(BF16)
    rank_all = jnp.dot(earlier, sel.astype(BF16), preferred_element_type=F32) + carry_ref[...]
    rank1 = jnp.sum(jnp.where(idx == i1, rank_all, 0.0), axis=1, keepdims=True)
    rank2 = jnp.sum(jnp.where(idx == i2, rank_all, 0.0), axis=1, keepdims=True)
    carry = carry_ref[...] + jnp.sum(sel, axis=0, keepdims=True)
    carry_ref[...] = carry
    cnt_ref[...] = carry.astype(jnp.int32)

    first = lax.broadcasted_iota(jnp.int32, (tm, TOP_K), 1) == 0
    idx_ref[...] = jnp.where(first, i1, i2)
    wgt_ref[...] = jnp.where(first, 1.0 / denom, e2 / denom)
    rank_ref[...] = jnp.where(first, rank1, rank2).astype(jnp.int32)


def _router(x, w_router, b_router, j):
    m, d = x.shape
    ne = w_router.shape[-1]
    tm = _tile(m, 520, V7X_BF16_SUBLANES)
    pair = pl.BlockSpec((tm, TOP_K), lambda i: (i, 0))
    return pl.pallas_call(
        _router_kernel, grid=(m // tm,),
        in_specs=[pl.BlockSpec((tm, d), lambda i: (i, 0)),
                  pl.BlockSpec((None, d, ne), lambda i: (j, 0, 0)),
                  pl.BlockSpec((None, 1, ne), lambda i: (j, 0, 0))],
        out_specs=[pair, pair, pair, pl.BlockSpec((1, ne), lambda i: (0, 0))],
        out_shape=[jax.ShapeDtypeStruct((m, TOP_K), jnp.int32), jax.ShapeDtypeStruct((m, TOP_K), F32),
                   jax.ShapeDtypeStruct((m, TOP_K), jnp.int32), jax.ShapeDtypeStruct((1, ne), jnp.int32)],
        scratch_shapes=[pltpu.VMEM((1, ne), F32)],
        compiler_params=_params(("arbitrary",)), name="router_top2")(x, w_router, b_router)


def _row_copy(src_hbm, src_row, dst, dst_row, sem):
    return pltpu.make_async_copy(src_hbm.at[pl.ds(src_row, 1)], dst.at[pl.ds(dst_row, 1)], sem)


def _dispatch_kernel(pos_ref, x_hbm, o_ref, inv_ref, buf, sem, *, n_tokens, tg):
    j = pl.program_id(0)
    nj = pl.num_programs(0)

    def start_tile(tile, slot):
        def body(r, c):
            _row_copy(x_hbm, inv_ref[tile * tg + r], buf.at[slot], r, sem.at[slot]).start()
            return c
        lax.fori_loop(0, tg, body, 0)

    @pl.when(j == 0)
    def _():
        def clear(p, c):
            inv_ref[p] = 0
            return c
        lax.fori_loop(0, nj * tg, clear, 0)

        def fill(t, c):
            for k in range(TOP_K):
                inv_ref[pos_ref[TOP_K * t + k]] = t
            return c
        lax.fori_loop(0, n_tokens, fill, 0)
        start_tile(0, 0)

    slot = j % 2

    @pl.when(j + 1 < nj)
    def _():
        start_tile(j + 1, 1 - slot)

    def wait(r, c):
        _row_copy(x_hbm, 0, buf.at[slot], r, sem.at[slot]).wait()
        return c
    lax.fori_loop(0, tg, wait, 0)
    o_ref[...] = buf[slot].astype(o_ref.dtype)


def _dispatch(pos_flat, x, n_slots, tg):
    m, d = x.shape
    grid_spec = pltpu.PrefetchScalarGridSpec(
        num_scalar_prefetch=1, grid=(n_slots // tg,),
        in_specs=[pl.BlockSpec(memory_space=pl.ANY)],
        out_specs=pl.BlockSpec((tg, d), lambda j, pos: (j, 0)),
        scratch_shapes=[pltpu.SMEM((n_slots,), jnp.int32), pltpu.VMEM((2, tg, d), F32),
                        pltpu.SemaphoreType.DMA((2,))])
    return pl.pallas_call(
        functools.partial(_dispatch_kernel, n_tokens=m, tg=tg), grid_spec=grid_spec,
        out_shape=jax.ShapeDtypeStruct((n_slots, d), BF16),
        compiler_params=_params(("arbitrary",)), name="moe_dispatch")(pos_flat, x)


def _combine_kernel(pos_ref, w_ref, y_hbm, o_ref, buf, sem, *, tc):
    i = pl.program_id(0)
    ni = pl.num_programs(0)

    def start_tile(tile, slot):
        def body(r, c):
            for k in range(TOP_K):
                _row_copy(y_hbm, pos_ref[TOP_K * (tile * tc + r) + k], buf.at[slot, k], r, sem.at[slot]).start()
            return c
        lax.fori_loop(0, tc, body, 0)

    @pl.when(i == 0)
    def _():
        start_tile(0, 0)

    slot = i % 2

    @pl.when(i + 1 < ni)
    def _():
        start_tile(i + 1, 1 - slot)

    def wait(r, c):
        for k in range(TOP_K):
            _row_copy(y_hbm, 0, buf.at[slot, k], r, sem.at[slot]).wait()
        return c
    lax.fori_loop(0, tc, wait, 0)
    w = w_ref[...]
    o_ref[...] = w[:, 0:1] * buf[slot, 0] + w[:, 1:2] * buf[slot, 1]


def _combine(pos_flat, weights, y, tc):
    m = weights.shape[0]
    d = y.shape[1]
    grid_spec = pltpu.PrefetchScalarGridSpec(
        num_scalar_prefetch=1, grid=(m // tc,),
        in_specs=[pl.BlockSpec((tc, TOP_K), lambda i, pos: (i, 0)), pl.BlockSpec(memory_space=pl.ANY)],
        out_specs=pl.BlockSpec((tc, d), lambda i, pos: (i, 0)),
        scratch_shapes=[pltpu.VMEM((2, TOP_K, tc, d), F32), pltpu.SemaphoreType.DMA((2,))])
    return pl.pallas_call(
        functools.partial(_combine_kernel, tc=tc), grid_spec=grid_spec,
        out_shape=jax.ShapeDtypeStruct((m, d), F32),
        compiler_params=_params(("arbitrary",)), name="moe_combine")(pos_flat, weights, y)


def _new_expert(te_ref, j):
    return (j == 0) | (te_ref[j] != te_ref[jnp.maximum(j - 1, 0)])


def _expert_up_kernel(te_ref, x_ref, wg_ref, wu_ref, o_ref, wgb_ref, wub_ref):
    @pl.when(_new_expert(te_ref, pl.program_id(1)))
    def _():
        wgb_ref[...] = wg_ref[...].astype(BF16)
        wub_ref[...] = wu_ref[...].astype(BF16)

    x = x_ref[...]
    a = jnp.dot(x, wgb_ref[...], preferred_element_type=F32)
    b = jnp.dot(x, wub_ref[...], preferred_element_type=F32)
    o_ref[...] = (a * jax.nn.sigmoid(a) * b).astype(o_ref.dtype)


def _expert_down_kernel(te_ref, x_ref, w_ref, o_ref, wb_ref):
    @pl.when(_new_expert(te_ref, pl.program_id(1)))
    def _():
        wb_ref[...] = w_ref[...].astype(BF16)

    o_ref[...] = jnp.dot(x_ref[...], wb_ref[...], preferred_element_type=F32)


def _expert_matmul(body, tile_expert, x, weights, layer, out_dtype, tn, tm, name):
    rows, k = x.shape
    n = weights[0].shape[-1]
    wspec = pl.BlockSpec((None, None, k, tn), lambda c, j, te: (layer, te[j], 0, c))
    grid_spec = pltpu.PrefetchScalarGridSpec(
        num_scalar_prefetch=1, grid=(n // tn, rows // tm),
        in_specs=[pl.BlockSpec((tm, k), lambda c, j, te: (j, 0))] + [wspec] * len(weights),
        out_specs=pl.BlockSpec((tm, tn), lambda c, j, te: (j, c)),
        scratch_shapes=[pltpu.VMEM((k, tn), BF16)] * len(weights))
    return pl.pallas_call(
        body, grid_spec=grid_spec, out_shape=jax.ShapeDtypeStruct((rows, n), out_dtype),
        compiler_params=_params(("arbitrary", "arbitrary")), name=name)(tile_expert, x, *weights)


def _moe(h_bf16, h_f32, w_router, b_router, w_gate, w_up, w_down, j):
    m, d = h_f32.shape
    ne = w_router.shape[-1]
    tm_e = MXU_TILE
    idx, wgt, rank, counts = _router(h_bf16, w_router, b_router, j)
    padded = (counts[0] + tm_e - 1) // tm_e * tm_e
    ends = jnp.cumsum(padded)
    starts = ends - padded
    n_tiles = -(-(TOP_K * m + ne * (tm_e - 1)) // tm_e)
    pos_flat = (starts[idx] + rank).reshape(-1)
    tile_expert = jnp.minimum(
        jnp.searchsorted(ends, jnp.arange(n_tiles, dtype=jnp.int32) * tm_e, side="right"), ne - 1
    ).astype(jnp.int32)
    xs = _dispatch(pos_flat, h_f32, n_tiles * tm_e, tm_e)
    act = _expert_matmul(_expert_up_kernel, tile_expert, xs, (w_gate, w_up), j, BF16,
                         _tile(w_gate.shape[-1], 512, V7X_LANES), tm_e, "expert_up")
    ys = _expert_matmul(_expert_down_kernel, tile_expert, act, (w_down,), j, F32,
                        _tile(d, 512, V7X_LANES), tm_e, "expert_down")
    return _combine(pos_flat, wgt, ys, _tile(m, 256, 8))


def _rope_tables(positions):
    half = QK_ROPE // 2
    inv = ROPE_BASE ** (-jnp.arange(half, dtype=F32) / half)
    ang = positions.astype(F32)[:, None] * inv
    cos, sin = jnp.cos(ang), jnp.sin(ang)
    reps = V7X_LANES // QK_ROPE
    return (jnp.tile(jnp.concatenate([cos, cos], axis=1), (1, reps)),
            jnp.tile(jnp.concatenate([-sin, sin], axis=1), (1, reps)))


def kernel(x_prompt, x_sample, cache_ckv, cache_krope, state_pool, state_lru_conv, state_lru_h, page_table, norm_mix_pre, norm_mix_post, norm_ffn_pre, norm_ffn_post, w_in, pool_w, pool_scale, lru_conv_w, lru_conv_b, lru_wa, lru_ba, lru_wx, lru_bx, lru_lambda, mla_q_norm, mla_w_uq, mla_kv_norm, mla_w_ukv, w_branch_pool, w_branch_lru, w_branch_mla, w_out, ffn_w_gate, ffn_w_up, ffn_w_down, moe_w_router, moe_b_router, moe_w_gate, moe_w_up, moe_w_down):
    batch, seq, d = x_prompt.shape
    nseq, dec_seq, _ = x_sample.shape
    assert dec_seq == 1
    depth = w_in.shape[0]
    pool_width = pool_scale.shape[1]
    lru_width = lru_lambda.shape[1]
    q_lora = mla_q_norm.shape[1]
    kv_lora = mla_kv_norm.shape[1]
    qk_head = QK_NOPE + QK_ROPE
    heads = mla_w_uq.shape[2] // qk_head
    past_len = page_table.shape[1] * cache_ckv.shape[2]
    scale = float(qk_head) ** -0.5
    mp = batch * seq
    m = mp + nseq
    o_lru_in = pool_width
    o_q = o_lru_in + lru_width
    o_kv = o_q + q_lora
    o_kr = o_kv + kv_lora
    o_gate = o_kr + QK_ROPE
    assert pool_width == lru_width == q_lora and o_kv % kv_lora == 0 and o_kr % V7X_LANES == 0
    assert mp % nseq == 0

    vec3 = lambda a: a.reshape(a.shape[0], 1, a.shape[1])
    pool_scale, lru_conv_b, lru_ba, lru_bx, lru_lambda, mla_q_norm, mla_kv_norm, moe_b_router = map(
        vec3, (pool_scale, lru_conv_b, lru_ba, lru_bx, lru_lambda, mla_q_norm, mla_kv_norm, moe_b_router))

    tm = _tile(m, 1040, V7X_BF16_SUBLANES)
    tm_small = _tile(m, 520, V7X_BF16_SUBLANES)
    tmp = _tile(mp, 1024, V7X_BF16_SUBLANES)
    tn = _tile(o_kr, 512, V7X_LANES)

    x = jnp.concatenate([x_prompt.reshape(mp, d), x_sample.reshape(nseq, d)], axis=0)
    positions = jnp.concatenate([jnp.tile(jnp.arange(seq), batch), jnp.full((nseq,), past_len)])
    cos, sin = _rope_tables(positions)
    w_in_t = jnp.swapaxes(w_in, 1, 2)
    cache_krope_t = jnp.swapaxes(cache_krope, 2, 3)

    outs_prompt = [[] for _ in range(5)]
    outs_sample = [[] for _ in range(5)]
    h = _norm(x, norm_mix_pre[0])
    for i in range(depth):
        z = _mm(h, w_in_t, prefix=(i,), ncols=o_kr, tn=tn, tm=tm, out_dtype=F32, name="in_proj",
                w_rows=True)
        z_kr = _mm(h, w_in_t, prefix=(i,), col0=o_kr, ncols=V7X_LANES, tn=V7X_LANES, tm=tm,
                   out_dtype=F32, name="in_proj_kr", w_rows=True)
        gate_logits = _mm(h, w_in_t, prefix=(i,), col0=o_gate, ncols=3 * d, tn=_tile(d, 512, V7X_LANES),
                          tm=tm, out_dtype=F32, name="in_proj_gates", w_rows=True)

        op_p = _pool_prompt(z, pool_w, pool_scale, i, batch, seq, pool_width)
        op_s = _pool_sample(z, state_pool[i].swapaxes(0, 1), pool_w, pool_scale, i, mp // nseq, nseq,
                            pool_width)
        o_pool = jnp.concatenate([op_p, op_s], axis=0)
        u_pool_p = z[:mp, :pool_width].reshape(batch, seq, pool_width)
        outs_prompt[2].append(u_pool_p[:, seq - state_pool.shape[2]:])
        outs_sample[2].append(jnp.concatenate([state_pool[i, :, 1:], z[mp:, None, :pool_width]], axis=1))

        chunk = min(MXU_TILE, lru_width)
        wa_bd = _block_diag(lru_wa[i], chunk)
        wx_bd = _block_diag(lru_wx[i], chunk)
        ol_p, hl_p = _lru_prompt(z, o_lru_in // lru_width, lru_conv_w, lru_conv_b, wa_bd, wx_bd,
                                 lru_ba, lru_bx, lru_lambda, i, batch, seq, lru_width)
        ol_s, hl_s = _lru_sample(z, mp // nseq, o_lru_in // lru_width, state_lru_conv[i].swapaxes(0, 1),
                                 state_lru_h, lru_conv_w, lru_conv_b, wa_bd, wx_bd, lru_ba, lru_bx,
                                 lru_lambda, i, nseq, lru_width)
        o_lru = jnp.concatenate([ol_p, ol_s], axis=0)
        u_lru = z[:, o_lru_in:o_q]
        outs_prompt[3].append(u_lru[:mp].reshape(batch, seq, lru_width)[:, seq - state_lru_conv.shape[2]:])
        outs_sample[3].append(jnp.concatenate([state_lru_conv[i, :, 1:], u_lru[mp:, None]], axis=1))
        outs_prompt[4].append(hl_p)
        outs_sample[4].append(hl_s)

        cq, ckv, ckv_b, kr, kr_b = _mla_prep(z, z_kr, cos, sin, mla_q_norm, mla_kv_norm, i,
                                             o_q // q_lora, o_kv // kv_lora, q_lora, kv_lora)
        outs_prompt[0].append(ckv[:mp].reshape(batch, seq, kv_lora))
        outs_prompt[1].append(kr[:mp].reshape(batch, seq, QK_ROPE))
        outs_sample[0].append(ckv[mp:].reshape(nseq, 1, kv_lora))
        outs_sample[1].append(kr[mp:].reshape(nseq, 1, QK_ROPE))
        w_uq = mla_w_uq[i].reshape(q_lora, heads, qk_head)
        w_uq_n = w_uq[:, :, :QK_NOPE].reshape(q_lora, heads * QK_NOPE)
        w_uq_r = w_uq[:, :, QK_NOPE:].reshape(q_lora, heads * QK_ROPE)
        qn = _mm(cq, w_uq_n, ncols=heads * QK_NOPE, tn=_tile(heads * QK_NOPE, 1024, V7X_LANES), tm=tm,
                 out_dtype=BF16, name="q_nope")
        qr = _mm(cq, w_uq_r, ncols=heads * QK_ROPE, tn=_tile(heads * QK_ROPE, 1024, V7X_LANES), tm=tm,
                 out_dtype=BF16, epilogue="rope", extra=(cos, sin), name="q_rope")
        w_kv = mla_w_ukv[i].reshape(kv_lora, heads, QK_NOPE + V_HEAD)
        w_k = w_kv[:, :, :QK_NOPE].reshape(kv_lora, heads * QK_NOPE)
        w_v = w_kv[:, :, QK_NOPE:].reshape(kv_lora, heads * V_HEAD)
        kn = _mm(ckv_b, w_k, ncols=heads * QK_NOPE, tn=_tile(heads * QK_NOPE, 1024, V7X_LANES), tm=tmp,
                 rows=mp, out_dtype=BF16, name="k_nope")
        vv = _mm(ckv_b, w_v, ncols=heads * V_HEAD, tn=_tile(heads * V_HEAD, 1024, V7X_LANES), tm=tmp,
                 rows=mp, out_dtype=BF16, name="v_up_prompt")
        om_p = _attn_prompt(qn, qr, kn, kr_b, vv, batch, seq, heads, scale)

        q_lat = _q_latent(qn, mla_w_ukv, i, mp // nseq, nseq, heads, kv_lora)
        o_lat = _paged_attention(page_table, q_lat.reshape(nseq, heads, kv_lora),
                                 qr[mp:].reshape(nseq, heads, QK_ROPE),
                                 ckv_b[mp:].reshape(nseq, 1, kv_lora), kr_b[mp:].reshape(nseq, 1, V7X_LANES),
                                 cache_ckv, cache_krope_t, i, scale)
        om_s = _v_up(o_lat.reshape(nseq, heads * kv_lora), mla_w_ukv, i, nseq, heads, kv_lora)
        o_mla = jnp.concatenate([om_p, om_s], axis=0)

        merged = _merge(o_pool, o_lru, o_mla, w_branch_pool, w_branch_lru, w_branch_mla, gate_logits, i,
                        tn=_tile(d, 512, V7X_LANES), tm=tm_small)
        mix = _mm(merged, w_out, prefix=(i,), ncols=d, tn=_tile(d, 512, V7X_LANES), tm=tm, out_dtype=F32,
                  name="out_proj")
        routed = i % 2 == 1
        x, h2, *h2_f32 = _resid_norm(x, mix, norm_mix_post[i], norm_ffn_pre[i], also_f32=routed)

        j = i // 2
        if not routed:
            d_ff = ffn_w_gate.shape[2]
            act = _swiglu_up(h2, ffn_w_gate, ffn_w_up, (j,), tn=_tile(d_ff, 256, V7X_LANES), tm=tm)
            f = _mm(act, ffn_w_down, prefix=(j,), ncols=d, tn=_tile(d, 256, V7X_LANES),
                    tm=_tile(m, 416, V7X_BF16_SUBLANES), out_dtype=F32, name="ffn_down")
        else:
            f = _moe(h2, h2_f32[0], moe_w_router, moe_b_router, moe_w_gate, moe_w_up, moe_w_down, j)
        g_next = norm_mix_pre[i + 1] if i + 1 < depth else None
        x, h = _resid_norm(x, f, norm_ffn_post[i], g_next)

    y_prompt = x[:mp].reshape(batch, seq, d)
    y_sample = x[mp:].reshape(nseq, 1, d)
    st = lambda lst: jnp.stack(lst)
    return (y_prompt, y_sample, st(outs_prompt[0]), st(outs_prompt[1]), st(outs_prompt[2]),
            st(outs_prompt[3]), st(outs_prompt[4]), st(outs_sample[0]), st(outs_sample[1]),
            st(outs_sample[2]), st(outs_sample[3]), st(outs_sample[4]))
```

```python
import functools

import jax
import jax.numpy as jnp
from jax import lax
from jax.experimental import pallas as pl
from jax.experimental.pallas import tpu as pltpu

BF16 = jnp.bfloat16
F32 = jnp.float32

V7X_LANES = 128
V7X_BF16_SUBLANES = 16
V7X_VMEM_LIMIT_BYTES = 56 * 1024 * 1024

POOL_WINDOWS = (2, 4, 8, 16)
LRU_C = 8.0
QK_NOPE = 128
QK_ROPE = 64
V_HEAD = 128
ROPE_BASE = 10000.0
TOP_K = 2
EPS = 1e-6
MXU_TILE = 256


def _tile(n, target, mult):
    best = None
    for t in range(mult, min(n, target) + 1, mult):
        if n % t == 0:
            best = t
    return best if best is not None else n


def _params(sem):
    return pltpu.CompilerParams(dimension_semantics=sem, vmem_limit_bytes=V7X_VMEM_LIMIT_BYTES)


def _wspec(w, k, tn, prefix, col0, kb=0):
    nlead = len(prefix)
    return pl.BlockSpec((None,) * nlead + (k, tn), lambda n, m: tuple(prefix) + (kb, n + col0))


def _wspec_t(k, tn, prefix, row0):
    nlead = len(prefix)
    return pl.BlockSpec((pl.Element(1),) * nlead + (pl.Element(tn), pl.Element(k)),
                        lambda n, m: tuple(prefix) + (pl.multiple_of(row0 + n * tn, 8), 0))


def _rms(x, g):
    return x * lax.rsqrt(jnp.mean(x * x, axis=-1, keepdims=True) + EPS) * g


def _norm_kernel(x_ref, g_ref, h_ref):
    h_ref[...] = _rms(x_ref[...], g_ref[...]).astype(h_ref.dtype)


def _resid_norm_kernel(x_ref, f_ref, gp_ref, gn_ref, xo_ref, *h_refs):
    x = x_ref[...] + _rms(f_ref[...], gp_ref[...])
    xo_ref[...] = x
    h = _rms(x, gn_ref[...])
    for h_ref in h_refs:
        h_ref[...] = h.astype(h_ref.dtype)


def _resid_kernel(x_ref, f_ref, gp_ref, xo_ref):
    xo_ref[...] = x_ref[...] + _rms(f_ref[...], gp_ref[...])


def _norm(x, g):
    m, d = x.shape
    tm = _tile(m, 256, V7X_BF16_SUBLANES)
    row = pl.BlockSpec((tm, d), lambda i: (i, 0))
    vec = pl.BlockSpec((1, d), lambda i: (0, 0))
    return pl.pallas_call(
        _norm_kernel, grid=(m // tm,), in_specs=[row, vec], out_specs=row,
        out_shape=jax.ShapeDtypeStruct((m, d), BF16), compiler_params=_params(("arbitrary",)),
        name="rms_norm")(x, g.reshape(1, d))


def _resid_norm(x, f, g_post, g_next, also_f32=False):
    m, d = x.shape
    tm = _tile(m, 128 if also_f32 else 256, V7X_BF16_SUBLANES)
    row = pl.BlockSpec((tm, d), lambda i: (i, 0))
    vec = pl.BlockSpec((1, d), lambda i: (0, 0))
    if g_next is None:
        return pl.pallas_call(
            _resid_kernel, grid=(m // tm,), in_specs=[row, row, vec], out_specs=row,
            out_shape=jax.ShapeDtypeStruct((m, d), F32), compiler_params=_params(("arbitrary",)),
            name="resid")(x, f, g_post.reshape(1, d)), None
    h_dtypes = (BF16, F32) if also_f32 else (BF16,)
    return pl.pallas_call(
        _resid_norm_kernel, grid=(m // tm,), in_specs=[row, row, vec, vec],
        out_specs=[row] * (1 + len(h_dtypes)),
        out_shape=[jax.ShapeDtypeStruct((m, d), F32)] + [jax.ShapeDtypeStruct((m, d), t) for t in h_dtypes],
        compiler_params=_params(("arbitrary",)),
        name="resid_norm")(x, f, g_post.reshape(1, d), g_next.reshape(1, d))


def _rope_tile(acc, cos_ref, sin_ref):
    tn = acc.shape[1]
    reps = tn // V7X_LANES
    cos = jnp.tile(cos_ref[...], (1, reps))
    sin = jnp.tile(sin_ref[...], (1, reps))
    lane = lax.broadcasted_iota(jnp.int32, acc.shape, 1) % QK_ROPE
    half = QK_ROPE // 2
    partner = jnp.where(lane < half, pltpu.roll(acc, tn - half, 1), pltpu.roll(acc, half, 1))
    return acc * cos + partner * sin


_NT = (((1,), (1,)), ((), ()))


def _mm_kernel(*refs, epilogue, w_rows):
    if epilogue == "rope":
        x_ref, w_ref, cos_ref, sin_ref, o_ref, wb_ref = refs
    elif epilogue == "add":
        x_ref, w_ref, a_ref, o_ref, wb_ref = refs
    else:
        x_ref, w_ref, o_ref, wb_ref = refs

    @pl.when(pl.program_id(1) == 0)
    def _():
        wb_ref[...] = w_ref[(0,) * (len(w_ref.shape) - 2)].astype(BF16)

    if w_rows:
        acc = lax.dot_general(x_ref[...], wb_ref[...], _NT, preferred_element_type=F32)
    else:
        acc = jnp.dot(x_ref[...], wb_ref[...], preferred_element_type=F32)
    if epilogue == "rope":
        acc = _rope_tile(acc, cos_ref, sin_ref)
    elif epilogue == "add":
        acc = a_ref[...] + acc
    o_ref[...] = acc.astype(o_ref.dtype)


def _mm(x, w, *, prefix=(), col0=0, ncols, tn, tm, out_dtype, rows=None, row0=0,
        epilogue="none", extra=(), name="mm", w_rows=False, k_block=None):
    m, k = x.shape
    kb, k = (0, k) if k_block is None else k_block
    rows = m if rows is None else rows
    assert rows % tm == 0 and ncols % tn == 0
    grid = (ncols // tn, rows // tm)
    wspec = _wspec_t(k, tn, prefix, col0) if w_rows else _wspec(w, k, tn, prefix, col0, kb)
    in_specs = [pl.BlockSpec((tm, k), lambda n, i: (i + row0, kb)), wspec]
    if epilogue == "rope":
        in_specs += [pl.BlockSpec((tm, V7X_LANES), lambda n, i: (i + row0, 0))] * 2
    elif epilogue == "add":
        in_specs += [pl.BlockSpec((tm, tn), lambda n, i: (i, n))]
    return pl.pallas_call(
        functools.partial(_mm_kernel, epilogue=epilogue, w_rows=w_rows), grid=grid, in_specs=in_specs,
        out_specs=pl.BlockSpec((tm, tn), lambda n, i: (i, n)),
        out_shape=jax.ShapeDtypeStruct((rows, ncols), out_dtype),
        scratch_shapes=[pltpu.VMEM((tn, k) if w_rows else (k, tn), BF16)],
        compiler_params=_params(("arbitrary", "arbitrary")), name=name)(x, w, *extra)


def _swiglu_kernel(x_ref, wg_ref, wu_ref, o_ref, wgb_ref, wub_ref):
    @pl.when(pl.program_id(1) == 0)
    def _():
        wgb_ref[...] = wg_ref[...].astype(BF16)
        wub_ref[...] = wu_ref[...].astype(BF16)

    x = x_ref[...]
    a = jnp.dot(x, wgb_ref[...], preferred_element_type=F32)
    b = jnp.dot(x, wub_ref[...], preferred_element_type=F32)
    o_ref[...] = (a * jax.nn.sigmoid(a) * b).astype(o_ref.dtype)


def _swiglu_up(x, wg, wu, prefix, *, tn, tm):
    m, k = x.shape
    n = wg.shape[-1]
    return pl.pallas_call(
        _swiglu_kernel, grid=(n // tn, m // tm),
        in_specs=[pl.BlockSpec((tm, k), lambda j, i: (i, 0)),
                  _wspec(wg, k, tn, prefix, 0), _wspec(wu, k, tn, prefix, 0)],
        out_specs=pl.BlockSpec((tm, tn), lambda j, i: (i, j)),
        out_shape=jax.ShapeDtypeStruct((m, n), BF16),
        scratch_shapes=[pltpu.VMEM((k, tn), BF16), pltpu.VMEM((k, tn), BF16)],
        compiler_params=_params(("arbitrary", "arbitrary")), name="swiglu_up")(x, wg, wu)


def _merge_kernel(op_ref, ol_ref, om_ref, wp_ref, wl_ref, wm_ref, gp_ref, gl_ref, gm_ref,
                  o_ref, wpb_ref, wlb_ref, wmb_ref):
    @pl.when(pl.program_id(1) == 0)
    def _():
        wpb_ref[...] = wp_ref[...].astype(BF16)
        wlb_ref[...] = wl_ref[...].astype(BF16)
        wmb_ref[...] = wm_ref[...].astype(BF16)

    acc = jax.nn.sigmoid(gp_ref[...]) * jnp.dot(op_ref[...], wpb_ref[...], preferred_element_type=F32)
    acc += jax.nn.sigmoid(gl_ref[...]) * jnp.dot(ol_ref[...], wlb_ref[...], preferred_element_type=F32)
    acc += jax.nn.sigmoid(gm_ref[...]) * jnp.dot(om_ref[...], wmb_ref[...], preferred_element_type=F32)
    o_ref[...] = acc.astype(o_ref.dtype)


def _merge(o_pool, o_lru, o_mla, wbp, wbl, wbm, gate_logits, layer, *, tn, tm):
    m = o_pool.shape[0]
    d = wbp.shape[-1]
    nb = d // tn
    lhs = lambda a: pl.BlockSpec((tm, a.shape[1]), lambda n, i: (i, 0))
    wsp = lambda w: _wspec(w, w.shape[-2], tn, (layer,), 0)
    gate = lambda g: pl.BlockSpec((tm, tn), lambda n, i: (i, n + g * nb))
    return pl.pallas_call(
        _merge_kernel, grid=(nb, m // tm),
        in_specs=[lhs(o_pool), lhs(o_lru), lhs(o_mla), wsp(wbp), wsp(wbl), wsp(wbm),
                  gate(0), gate(1), gate(2)],
        out_specs=pl.BlockSpec((tm, tn), lambda n, i: (i, n)),
        out_shape=jax.ShapeDtypeStruct((m, d), BF16),
        scratch_shapes=[pltpu.VMEM((wbp.shape[-2], tn), BF16), pltpu.VMEM((wbl.shape[-2], tn), BF16),
                        pltpu.VMEM((wbm.shape[-2], tn), BF16)],
        compiler_params=_params(("arbitrary", "arbitrary")),
        name="gated_merge")(o_pool, o_lru, o_mla, wbp, wbl, wbm, gate_logits, gate_logits, gate_logits)


def _pool_project(sums, u, cnts, pw_ref, ps_ref, group):
    outs = []
    for g in range(len(POOL_WINDOWS)):
        sl = slice(g * group, (g + 1) * group)
        pooled = sums[g][:, sl] / cnts[g] - u[:, sl]
        outs.append(jnp.dot(pooled.astype(BF16), pw_ref[g].astype(BF16), preferred_element_type=F32))
    return jnp.concatenate(outs, axis=1) * ps_ref[...]


def _pool_prompt_kernel(u_ref, halo_ref, pw_ref, ps_ref, o_ref, *, tt, group):
    t = pl.program_id(1)
    halo_rows = halo_ref.shape[0]
    halo = jnp.where(t == 0, 0.0, halo_ref[...])
    u = u_ref[...]
    full = jnp.concatenate([halo, u], axis=0)
    sums = []
    s = full
    span = 1
    for win in POOL_WINDOWS:
        while span < win:
            s = s + pltpu.roll(s, span, 0)
            span *= 2
        sums.append(s[halo_rows:])
    pos = t * tt + lax.broadcasted_iota(jnp.int32, (tt, 1), 0)
    cnts = [jnp.minimum(pos + 1, win).astype(F32) for win in POOL_WINDOWS]
    o_ref[...] = _pool_project(sums, u, cnts, pw_ref, ps_ref, group).astype(o_ref.dtype)


def _pool_prompt(z, pool_w, pool_scale, layer, batch, seq, width):
    tt = _tile(seq, 256, V7X_BF16_SUBLANES)
    halo = 16
    nt = seq // tt
    group = width // len(POOL_WINDOWS)
    return pl.pallas_call(
        functools.partial(_pool_prompt_kernel, tt=tt, group=group), grid=(batch, nt),
        in_specs=[pl.BlockSpec((tt, width), lambda b, t: (b * nt + t, 0)),
                  pl.BlockSpec((halo, width),
                               lambda b, t: (jnp.maximum((b * nt + t) * (tt // halo) - 1, 0), 0)),
                  pl.BlockSpec((None,) + pool_w.shape[1:], lambda b, t: (layer, 0, 0, 0)),
                  pl.BlockSpec((None, 1, width), lambda b, t: (layer, 0, 0))],
        out_specs=pl.BlockSpec((tt, width), lambda b, t: (b * nt + t, 0)),
        out_shape=jax.ShapeDtypeStruct((batch * seq, width), BF16),
        compiler_params=_params(("arbitrary", "arbitrary")), name="pool_prompt")(z, z, pool_w, pool_scale)


def _pool_sample_kernel(u_ref, st_ref, pw_ref, ps_ref, o_ref, *, group):
    u = u_ref[...]
    nbuf = st_ref.shape[0]
    sums = []
    s = u
    taken = 0
    for win in POOL_WINDOWS:
        while taken < win - 1:
            taken += 1
            s = s + st_ref[nbuf - taken]
        sums.append(s)
    cnts = [float(win) for win in POOL_WINDOWS]
    o_ref[...] = _pool_project(sums, u, cnts, pw_ref, ps_ref, group).astype(o_ref.dtype)


def _pool_sample(z, state_t, pool_w, pool_scale, layer, row_block, nrows, width):
    group = width // len(POOL_WINDOWS)
    return pl.pallas_call(
        functools.partial(_pool_sample_kernel, group=group), grid=(1,),
        in_specs=[pl.BlockSpec((nrows, width), lambda i: (row_block, 0)),
                  pl.BlockSpec(state_t.shape, lambda i: (0, 0, 0)),
                  pl.BlockSpec((None,) + pool_w.shape[1:], lambda i: (layer, 0, 0, 0)),
                  pl.BlockSpec((None, 1, width), lambda i: (layer, 0, 0))],
        out_specs=pl.BlockSpec((nrows, width), lambda i: (0, 0)),
        out_shape=jax.ShapeDtypeStruct((nrows, width), BF16),
        compiler_params=_params(("arbitrary",)), name="pool_sample")(z, state_t, pool_w, pool_scale)


def _lru_gates(xc, wa_ref, wx_ref, ba_ref, bx_ref, lam_ref):
    nchunk, cw, _ = wa_ref.shape
    xb = xc.astype(BF16)
    ra, rx = [], []
    for c in range(nchunk):
        xs = xb[:, c * cw:(c + 1) * cw]
        ra.append(jnp.dot(xs, wa_ref[c].astype(BF16), preferred_element_type=F32))
        rx.append(jnp.dot(xs, wx_ref[c].astype(BF16), preferred_element_type=F32))
    r = jax.nn.sigmoid(jnp.concatenate(ra, axis=1) + ba_ref[...])
    gi = jax.nn.sigmoid(jnp.concatenate(rx, axis=1) + bx_ref[...])
    nl = -lam_ref[...]
    softplus = jnp.maximum(nl, 0.0) + jnp.log1p(jnp.exp(-jnp.abs(nl)))
    log_a = -LRU_C * r * softplus
    a = jnp.exp(log_a)
    b = jnp.sqrt(jnp.tanh(-log_a) * (a * a + 1.0)) * (gi * xc)
    return a, b


def _lru_prompt_kernel(u_ref, cw_ref, cb_ref, wa_ref, wx_ref, ba_ref, bx_ref, lam_ref,
                       o_ref, hl_ref, tail_ref, h_ref, *, tc):
    t = pl.program_id(1)
    keep = tail_ref.shape[0]

    @pl.when(t == 0)
    def _():
        tail_ref[...] = jnp.zeros_like(tail_ref)
        h_ref[...] = jnp.zeros_like(h_ref)

    u = u_ref[...]
    full = jnp.concatenate([tail_ref[...], u], axis=0)
    taps = cw_ref.shape[0]
    xc = cb_ref[...] + cw_ref[taps - 1:taps, :] * u
    for k in range(1, taps):
        xc = xc + cw_ref[taps - 1 - k:taps - k, :] * pltpu.roll(full, k, 0)[keep:]
    tail_ref[...] = u[tc - keep:]

    a, b = _lru_gates(xc, wa_ref, wx_ref, ba_ref, bx_ref, lam_ref)
    row = lax.broadcasted_iota(jnp.int32, (tc, 1), 0)
    d = 1
    while d < tc:
        valid = row >= d
        a_prev = jnp.where(valid, pltpu.roll(a, d, 0), 1.0)
        b_prev = jnp.where(valid, pltpu.roll(b, d, 0), 0.0)
        b = a * b_prev + b
        a = a * a_prev
        d *= 2
    hs = a * h_ref[...] + b
    h_last = hs[tc - 1:tc]
    h_ref[...] = h_last
    hl_ref[...] = h_last
    o_ref[...] = hs.astype(o_ref.dtype)


def _lru_prompt(z, col_block, conv_w, conv_b, wa_bd, wx_bd, ba, bx, lam, layer, batch, seq, width):
    tc = _tile(seq, 256, V7X_BF16_SUBLANES)
    nt = seq // tc
    vec = pl.BlockSpec((None, 1, width), lambda b, t: (layer, 0, 0))
    bd = pl.BlockSpec(wa_bd.shape, lambda b, t: (0, 0, 0))
    o, h_last = pl.pallas_call(
        functools.partial(_lru_prompt_kernel, tc=tc), grid=(batch, nt),
        in_specs=[pl.BlockSpec((tc, width), lambda b, t: (b * nt + t, col_block)),
                  pl.BlockSpec((None,) + conv_w.shape[1:], lambda b, t: (layer, 0, 0)),
                  vec, bd, bd, vec, vec, vec],
        out_specs=[pl.BlockSpec((tc, width), lambda b, t: (b * nt + t, 0)),
                   pl.BlockSpec((None, 1, width), lambda b, t: (b, 0, 0))],
        out_shape=[jax.ShapeDtypeStruct((batch * seq, width), BF16),
                   jax.ShapeDtypeStruct((batch, 1, width), F32)],
        scratch_shapes=[pltpu.VMEM((8, width), F32), pltpu.VMEM((1, width), F32)],
        compiler_params=_params(("arbitrary", "arbitrary")),
        name="lru_prompt")(z, conv_w, conv_b, wa_bd, wx_bd, ba, bx, lam)
    return o, h_last.reshape(batch, width)


def _lru_sample_kernel(u_ref, cs_ref, h0_ref, cw_ref, cb_ref, wa_ref, wx_ref, ba_ref, bx_ref, lam_ref,
                       o_ref, hn_ref):
    taps = cw_ref.shape[0]
    xc = cb_ref[...] + cw_ref[taps - 1:taps, :] * u_ref[...]
    for k in range(taps - 1):
        xc = xc + cw_ref[k:k + 1, :] * cs_ref[k]
    a, b = _lru_gates(xc, wa_ref, wx_ref, ba_ref, bx_ref, lam_ref)
    h = a * h0_ref[...] + b
    hn_ref[...] = h
    o_ref[...] = h.astype(o_ref.dtype)


def _lru_sample(z, row_block, col_block, conv_t, h0, conv_w, conv_b, wa_bd, wx_bd, ba, bx, lam,
                layer, nrows, width):
    vec = pl.BlockSpec((None, 1, width), lambda i: (layer, 0, 0))
    bd = pl.BlockSpec(wa_bd.shape, lambda i: (0, 0, 0))
    return pl.pallas_call(
        _lru_sample_kernel, grid=(1,),
        in_specs=[pl.BlockSpec((nrows, width), lambda i: (row_block, col_block)),
                  pl.BlockSpec(conv_t.shape, lambda i: (0, 0, 0)),
                  pl.BlockSpec((None, nrows, width), lambda i: (layer, 0, 0)),
                  pl.BlockSpec((None,) + conv_w.shape[1:], lambda i: (layer, 0, 0)),
                  vec, bd, bd, vec, vec, vec],
        out_specs=[pl.BlockSpec((nrows, width), lambda i: (0, 0))] * 2,
        out_shape=[jax.ShapeDtypeStruct((nrows, width), BF16), jax.ShapeDtypeStruct((nrows, width), F32)],
        compiler_params=_params(("arbitrary",)),
        name="lru_sample")(z, conv_t, h0, conv_w, conv_b, wa_bd, wx_bd, ba, bx, lam)


def _block_diag(w, chunk):
    heads, blk, _ = w.shape
    per = chunk // blk
    eye = jnp.eye(per, dtype=w.dtype)
    wr = w.reshape(heads // per, per, blk, blk)
    return jnp.einsum("cpij,pq->cpiqj", wr, eye).reshape(heads // per, chunk, chunk)


def _mla_prep_kernel(cq_ref, ckv_ref, kr_ref, cos_ref, sin_ref, gq_ref, gkv_ref,
                     cqn_ref, ckvn_ref, ckvb_ref, krn_ref, krb_ref):
    cqn_ref[...] = _rms(cq_ref[...], gq_ref[...]).astype(cqn_ref.dtype)
    ckv = _rms(ckv_ref[...], gkv_ref[...])
    ckvn_ref[...] = ckv
    ckvb_ref[...] = ckv.astype(ckvb_ref.dtype)
    x = kr_ref[...]
    lane = lax.broadcasted_iota(jnp.int32, x.shape, 1)
    half = QK_ROPE // 2
    partner = jnp.where(lane % QK_ROPE < half, pltpu.roll(x, V7X_LANES - half, 1), pltpu.roll(x, half, 1))
    kr = jnp.where(lane < QK_ROPE, x * cos_ref[...] + partner * sin_ref[...], 0.0)
    krn_ref[...] = kr[:, :QK_ROPE]
    krb_ref[...] = (kr + pltpu.roll(kr, QK_ROPE, 1)).astype(krb_ref.dtype)


def _mla_prep(z, z_kr, cos, sin, q_norm, kv_norm, layer, q_block, kv_block, q_lora, kv_lora):
    m = z.shape[0]
    tm = _tile(m, 512, V7X_BF16_SUBLANES)
    rows = lambda w, c: pl.BlockSpec((tm, w), lambda i: (i, c))
    return pl.pallas_call(
        _mla_prep_kernel, grid=(m // tm,),
        in_specs=[rows(q_lora, q_block), rows(kv_lora, kv_block), rows(V7X_LANES, 0),
                  rows(V7X_LANES, 0), rows(V7X_LANES, 0),
                  pl.BlockSpec((None, 1, q_lora), lambda i: (layer, 0, 0)),
                  pl.BlockSpec((None, 1, kv_lora), lambda i: (layer, 0, 0))],
        out_specs=[rows(q_lora, 0), rows(kv_lora, 0), rows(kv_lora, 0), rows(QK_ROPE, 0),
                   rows(V7X_LANES, 0)],
        out_shape=[jax.ShapeDtypeStruct((m, q_lora), BF16), jax.ShapeDtypeStruct((m, kv_lora), F32),
                   jax.ShapeDtypeStruct((m, kv_lora), BF16), jax.ShapeDtypeStruct((m, QK_ROPE), F32),
                   jax.ShapeDtypeStruct((m, V7X_LANES), BF16)],
        compiler_params=_params(("arbitrary",)), name="mla_prep")(z, z, z_kr, cos, sin, q_norm, kv_norm)


ATTN_HEADS_PER_STEP = 4


def _attn_prompt_kernel(qn_ref, qr_ref, kn_ref, kr_ref, v_ref, o_ref, *, tq, tk, scale):
    qi = pl.program_id(2)
    lane = lax.broadcasted_iota(jnp.int32, (tq, V7X_LANES), 1)
    qpos = qi * tq + lax.broadcasted_iota(jnp.int32, (tq, tk), 0)
    kidx = lax.broadcasted_iota(jnp.int32, (tq, tk), 1)
    queries = []
    for hh in range(ATTN_HEADS_PER_STEP):
        pair = qr_ref[:, (hh // 2) * V7X_LANES:(hh // 2 + 1) * V7X_LANES]
        in_head = (lane >= (hh % 2) * QK_ROPE) & (lane < (hh % 2 + 1) * QK_ROPE)
        qr = jnp.where(in_head, pair, jnp.zeros_like(pair))
        queries.append(jnp.concatenate([qn_ref[:, hh * QK_NOPE:(hh + 1) * QK_NOPE], qr], axis=1))

    def body(kb, carry):
        off = pl.multiple_of(kb * tk, tk)
        kr = kr_ref[pl.ds(off, tk), :]
        visible = kidx + off <= qpos
        new = []
        for hh, (m, l, acc) in enumerate(carry):
            hs = slice(hh * QK_NOPE, (hh + 1) * QK_NOPE)
            keys = jnp.concatenate([kn_ref[pl.ds(off, tk), hs], kr], axis=1)
            s = lax.dot_general(queries[hh], keys, _NT, preferred_element_type=F32) * scale
            s = jnp.where(visible, s, -jnp.inf)
            m_new = jnp.maximum(m, jnp.max(s, axis=1, keepdims=True))
            corr = jnp.exp(m - m_new)
            p = jnp.exp(s - m_new)
            l = l * corr + jnp.sum(p, axis=1, keepdims=True)
            acc = acc * corr + jnp.dot(p.astype(BF16), v_ref[pl.ds(off, tk), hs], preferred_element_type=F32)
            new.append((m_new, l, acc))
        return tuple(new)

    init = tuple((jnp.full((tq, 1), -jnp.inf, F32), jnp.zeros((tq, 1), F32), jnp.zeros((tq, V_HEAD), F32))
                 for _ in range(ATTN_HEADS_PER_STEP))
    nkb = (qi * tq + tq + tk - 1) // tk
    final = lax.fori_loop(0, nkb, body, init)
    o_ref[...] = jnp.concatenate([acc / l for _, l, acc in final], axis=1).astype(o_ref.dtype)


def _attn_prompt(qn, qr, kn, krb, v, batch, seq, heads, scale):
    tq = _tile(seq, 256, V7X_BF16_SUBLANES)
    tk = tq
    nq = seq // tq
    hps = ATTN_HEADS_PER_STEP
    wide = hps * QK_NOPE
    return pl.pallas_call(
        functools.partial(_attn_prompt_kernel, tq=tq, tk=tk, scale=scale), grid=(batch, heads // hps, nq),
        in_specs=[pl.BlockSpec((tq, wide), lambda b, h, q: (b * nq + q, h)),
                  pl.BlockSpec((tq, hps * QK_ROPE), lambda b, h, q: (b * nq + q, h)),
                  pl.BlockSpec((seq, wide), lambda b, h, q: (b, h)),
                  pl.BlockSpec((seq, V7X_LANES), lambda b, h, q: (b, 0)),
                  pl.BlockSpec((seq, wide), lambda b, h, q: (b, h))],
        out_specs=pl.BlockSpec((tq, wide), lambda b, h, q: (b * nq + q, h)),
        out_shape=jax.ShapeDtypeStruct((batch * seq, heads * V_HEAD), BF16),
        compiler_params=_params(("arbitrary", "arbitrary", "arbitrary")),
        name="attn_prompt")(qn, qr, kn, krb, v)


def _head_map_kernel(x_ref, w_ref, o_ref, *, transpose_w):
    w = w_ref[...].astype(BF16)
    dims = (((1,), (1,)), ((), ())) if transpose_w else (((1,), (0,)), ((), ()))
    o_ref[...] = lax.dot_general(x_ref[...], w, dims, preferred_element_type=F32).astype(o_ref.dtype)


def _q_latent(qn, w_ukv, layer, row_block, nrows, heads, kv_lora):
    return pl.pallas_call(
        functools.partial(_head_map_kernel, transpose_w=True), grid=(heads,),
        in_specs=[pl.BlockSpec((nrows, QK_NOPE), lambda h: (row_block, h)),
                  pl.BlockSpec((None, kv_lora, QK_NOPE), lambda h: (layer, 0, 2 * h))],
        out_specs=pl.BlockSpec((nrows, kv_lora), lambda h: (0, h)),
        out_shape=jax.ShapeDtypeStruct((nrows, heads * kv_lora), BF16),
        compiler_params=_params(("arbitrary",)), name="q_latent")(qn, w_ukv)


def _v_up(o_lat, w_ukv, layer, nrows, heads, kv_lora):
    return pl.pallas_call(
        functools.partial(_head_map_kernel, transpose_w=False), grid=(heads,),
        in_specs=[pl.BlockSpec((nrows, kv_lora), lambda h: (0, h)),
                  pl.BlockSpec((None, kv_lora, V_HEAD), lambda h: (layer, 0, 2 * h + 1))],
        out_specs=pl.BlockSpec((nrows, V_HEAD), lambda h: (0, h)),
        out_shape=jax.ShapeDtypeStruct((nrows, heads * V_HEAD), BF16),
        compiler_params=_params(("arbitrary",)), name="v_up")(o_lat, w_ukv)


PAGED_SLOTS = 3


def _paged_kernel(pt_ref, ql_ref, qr_ref, cs_ref, ks_ref, ck_hbm, kp_hbm, o_ref, ck_buf, kp_buf, sem,
                  *, layer, group, n_groups, n_seq, scale):
    b = pl.program_id(0)
    total = n_seq * n_groups
    n_slots, _, page, kv_lora = ck_buf.shape
    ahead = n_slots - 1

    def group_copies(k):
        seq = k // n_groups
        first = (k % n_groups) * group
        slot = k % n_slots
        copies = []
        for j in range(group):
            phys = pt_ref[seq, first + j]
            copies.append(pltpu.make_async_copy(ck_hbm.at[layer, phys], ck_buf.at[slot, j], sem.at[slot, 0]))
            copies.append(pltpu.make_async_copy(kp_hbm.at[layer, phys], kp_buf.at[slot, j], sem.at[slot, 1]))
        return copies

    @pl.when(b == 0)
    def _():
        for k in range(min(ahead, total)):
            for c in group_copies(k):
                c.start()

    ql = ql_ref[...]
    qr = qr_ref[...]

    def body(g, carry):
        m, l, acc = carry
        k = b * n_groups + g
        slot = k % n_slots

        @pl.when(k + ahead < total)
        def _():
            for c in group_copies(k + ahead):
                c.start()

        for c in group_copies(k):
            c.wait()

        ck = ck_buf[slot].reshape(group * page, kv_lora).astype(BF16)
        s_rope = jnp.concatenate(
            [jnp.dot(qr, kp_buf[slot, j].astype(BF16), preferred_element_type=F32) for j in range(group)],
            axis=1)
        s = (lax.dot_general(ql, ck, _NT, preferred_element_type=F32) + s_rope) * scale
        m_new = jnp.maximum(m, jnp.max(s, axis=1, keepdims=True))
        corr = jnp.exp(m - m_new)
        pr = jnp.exp(s - m_new)
        l = l * corr + jnp.sum(pr, axis=1, keepdims=True)
        acc = acc * corr + jnp.dot(pr.astype(BF16), ck, preferred_element_type=F32)
        return m_new, l, acc

    heads = ql.shape[0]
    init = (jnp.full((heads, 1), -jnp.inf, F32), jnp.zeros((heads, 1), F32),
            jnp.zeros((heads, kv_lora), F32))
    m, l, acc = lax.fori_loop(0, n_groups, body, init)

    cs = cs_ref[...].astype(F32)
    ks = ks_ref[...][:, :QK_ROPE].astype(F32)
    s = (jnp.sum(ql.astype(F32) * cs, axis=1, keepdims=True)
         + jnp.sum(qr.astype(F32) * ks, axis=1, keepdims=True)) * scale
    m_new = jnp.maximum(m, s)
    corr = jnp.exp(m - m_new)
    pr = jnp.exp(s - m_new)
    l_fin = l * corr + pr
    acc_fin = acc * corr + pr.astype(BF16).astype(F32) * cs
    o_ref[...] = (acc_fin / l_fin).astype(o_ref.dtype)


def _paged_attention(page_table, q_lat, q_rope, ckv_self, kr_self, cache_ckv, cache_krope_t, layer, scale):
    nseq, heads, kv_lora = q_lat.shape
    n_pages = page_table.shape[1]
    page = cache_ckv.shape[2]
    group = _tile(n_pages, 16, 1)
    n_groups = n_pages // group
    seq_spec = lambda a: pl.BlockSpec((None,) + a.shape[1:], lambda b, pt: (b, 0, 0))
    grid_spec = pltpu.PrefetchScalarGridSpec(
        num_scalar_prefetch=1, grid=(nseq,),
        in_specs=[seq_spec(q_lat), seq_spec(q_rope), seq_spec(ckv_self), seq_spec(kr_self),
                  pl.BlockSpec(memory_space=pl.ANY), pl.BlockSpec(memory_space=pl.ANY)],
        out_specs=pl.BlockSpec((None, heads, kv_lora), lambda b, pt: (b, 0, 0)),
        scratch_shapes=[pltpu.VMEM((PAGED_SLOTS, group, page, kv_lora), F32),
                        pltpu.VMEM((PAGED_SLOTS, group, QK_ROPE, page), F32),
                        pltpu.SemaphoreType.DMA((PAGED_SLOTS, 2))])
    return pl.pallas_call(
        functools.partial(_paged_kernel, layer=layer, group=group, n_groups=n_groups, n_seq=nseq,
                          scale=scale),
        grid_spec=grid_spec,
        out_shape=jax.ShapeDtypeStruct((nseq, heads, kv_lora), BF16),
        compiler_params=_params(("arbitrary",)),
        name="paged_attention")(page_table, q_lat, q_rope, ckv_self, kr_self, cache_ckv, cache_krope_t)


def _router_kernel(x_ref, w_ref, b_ref, idx_ref, wgt_ref, rank_ref, cnt_ref, carry_ref):
    @pl.when(pl.program_id(0) == 0)
    def _():
        carry_ref[...] = jnp.zeros_like(carry_ref)

    logits = jnp.dot(x_ref[...], w_ref[...].astype(BF16), preferred_element_type=F32) + b_ref[...]
    tm, ne = logits.shape
    idx = lax.broadcasted_iota(jnp.int32, logits.shape, 1)
    v1 = jnp.max(logits, axis=1, keepdims=True)
    i1 = jnp.min(jnp.where(logits == v1, idx, ne), axis=1, keepdims=True)
    rest = jnp.where(idx == i1, -jnp.inf, logits)
    v2 = jnp.max(rest, axis=1, keepdims=True)
    i2 = jnp.min(jnp.where(rest == v2, idx, ne), axis=1, keepdims=True)
    e2 = jnp.exp(v2 - v1)
    denom = 1.0 + e2

    sel = ((idx == i1) | (idx == i2)).astype(F32)
    earlier = (lax.broadcasted_iota(jnp.int32, (tm, tm), 1)
               < lax.broadcasted_iota(jnp.int32, (tm, tm), 0)).astype(BF16)
    rank_all = jnp.dot(earlier, sel.astype(BF16), preferred_element_type=F32) + carry_ref[...]
    rank1 = jnp.sum(jnp.where(idx == i1, rank_all, 0.0), axis=1, keepdims=True)
    rank2 = jnp.sum(jnp.where(idx == i2, rank_all, 0.0), axis=1, keepdims=True)
    carry = carry_ref[...] + jnp.sum(sel, axis=0, keepdims=True)
    carry_ref[...] = carry
    cnt_ref[...] = carry.astype(jnp.int32)

    first = lax.broadcasted_iota(jnp.int32, (tm, TOP_K), 1) == 0
    idx_ref[...] = jnp.where(first, i1, i2)
    wgt_ref[...] = jnp.where(first, 1.0 / denom, e2 / denom)
    rank_ref[...] = jnp.where(first, rank1, rank2).astype(jnp.int32)


def _router(x, w_router, b_router, j):
    m, d = x.shape
    ne = w_router.shape[-1]
    tm = _tile(m, 520, V7X_BF16_SUBLANES)
    pair = pl.BlockSpec((tm, TOP_K), lambda i: (i, 0))
    return pl.pallas_call(
        _router_kernel, grid=(m // tm,),
        in_specs=[pl.BlockSpec((tm, d), lambda i: (i, 0)),
                  pl.BlockSpec((None, d, ne), lambda i: (j, 0, 0)),
                  pl.BlockSpec((None, 1, ne), lambda i: (j, 0, 0))],
        out_specs=[pair, pair, pair, pl.BlockSpec((1, ne), lambda i: (0, 0))],
        out_shape=[jax.ShapeDtypeStruct((m, TOP_K), jnp.int32), jax.ShapeDtypeStruct((m, TOP_K), F32),
                   jax.ShapeDtypeStruct((m, TOP_K), jnp.int32), jax.ShapeDtypeStruct((1, ne), jnp.int32)],
        scratch_shapes=[pltpu.VMEM((1, ne), F32)],
        compiler_params=_params(("arbitrary",)), name="router_top2")(x, w_router, b_router)


def _row_copy(src_hbm, src_row, dst, dst_row, sem):
    return pltpu.make_async_copy(src_hbm.at[pl.ds(src_row, 1)], dst.at[pl.ds(dst_row, 1)], sem)


def _dispatch_kernel(pos_ref, x_hbm, o_ref, inv_ref, buf, sem, *, n_tokens, tg):
    j = pl.program_id(0)
    nj = pl.num_programs(0)

    def start_tile(tile, slot):
        def body(r, c):
            _row_copy(x_hbm, inv_ref[tile * tg + r], buf.at[slot], r, sem.at[slot]).start()
            return c
        lax.fori_loop(0, tg, body, 0)

    @pl.when(j == 0)
    def _():
        def clear(p, c):
            inv_ref[p] = 0
            return c
        lax.fori_loop(0, nj * tg, clear, 0)

        def fill(t, c):
            for k in range(TOP_K):
                inv_ref[pos_ref[TOP_K * t + k]] = t
            return c
        lax.fori_loop(0, n_tokens, fill, 0)
        start_tile(0, 0)

    slot = j % 2

    @pl.when(j + 1 < nj)
    def _():
        start_tile(j + 1, 1 - slot)

    def wait(r, c):
        _row_copy(x_hbm, 0, buf.at[slot], r, sem.at[slot]).wait()
        return c
    lax.fori_loop(0, tg, wait, 0)
    o_ref[...] = buf[slot].astype(o_ref.dtype)


def _dispatch(pos_flat, x, n_slots, tg):
    m, d = x.shape
    grid_spec = pltpu.PrefetchScalarGridSpec(
        num_scalar_prefetch=1, grid=(n_slots // tg,),
        in_specs=[pl.BlockSpec(memory_space=pl.ANY)],
        out_specs=pl.BlockSpec((tg, d), lambda j, pos: (j, 0)),
        scratch_shapes=[pltpu.SMEM((n_slots,), jnp.int32), pltpu.VMEM((2, tg, d), F32),
                        pltpu.SemaphoreType.DMA((2,))])
    return pl.pallas_call(
        functools.partial(_dispatch_kernel, n_tokens=m, tg=tg), grid_spec=grid_spec,
        out_shape=jax.ShapeDtypeStruct((n_slots, d), BF16),
        compiler_params=_params(("arbitrary",)), name="moe_dispatch")(pos_flat, x)


def _combine_kernel(pos_ref, w_ref, y_hbm, o_ref, buf, sem, *, tc):
    i = pl.program_id(0)
    ni = pl.num_programs(0)

    def start_tile(tile, slot):
        def body(r, c):
            for k in range(TOP_K):
                _row_copy(y_hbm, pos_ref[TOP_K * (tile * tc + r) + k], buf.at[slot, k], r, sem.at[slot]).start()
            return c
        lax.fori_loop(0, tc, body, 0)

    @pl.when(i == 0)
    def _():
        start_tile(0, 0)

    slot = i % 2

    @pl.when(i + 1 < ni)
    def _():
        start_tile(i + 1, 1 - slot)

    def wait(r, c):
        for k in range(TOP_K):
            _row_copy(y_hbm, 0, buf.at[slot, k], r, sem.at[slot]).wait()
        return c
    lax.fori_loop(0, tc, wait, 0)
    w = w_ref[...]
    o_ref[...] = w[:, 0:1] * buf[slot, 0] + w[:, 1:2] * buf[slot, 1]


def _combine(pos_flat, weights, y, tc):
    m = weights.shape[0]
    d = y.shape[1]
    grid_spec = pltpu.PrefetchScalarGridSpec(
        num_scalar_prefetch=1, grid=(m // tc,),
        in_specs=[pl.BlockSpec((tc, TOP_K), lambda i, pos: (i, 0)), pl.BlockSpec(memory_space=pl.ANY)],
        out_specs=pl.BlockSpec((tc, d), lambda i, pos: (i, 0)),
        scratch_shapes=[pltpu.VMEM((2, TOP_K, tc, d), F32), pltpu.SemaphoreType.DMA((2,))])
    return pl.pallas_call(
        functools.partial(_combine_kernel, tc=tc), grid_spec=grid_spec,
        out_shape=jax.ShapeDtypeStruct((m, d), F32),
        compiler_params=_params(("arbitrary",)), name="moe_combine")(pos_flat, weights, y)


def _new_expert(te_ref, j):
    return (j == 0) | (te_ref[j] != te_ref[jnp.maximum(j - 1, 0)])


def _expert_up_kernel(te_ref, nv_ref, x_ref, wg_ref, wu_ref, o_ref, wgb_ref, wub_ref):
    j = pl.program_id(1)
    used = j < nv_ref[0]

    @pl.when(used & _new_expert(te_ref, j))
    def _():
        wgb_ref[...] = wg_ref[...].astype(BF16)
        wub_ref[...] = wu_ref[...].astype(BF16)

    @pl.when(used)
    def _():
        x = x_ref[...]
        a = jnp.dot(x, wgb_ref[...], preferred_element_type=F32)
        b = jnp.dot(x, wub_ref[...], preferred_element_type=F32)
        o_ref[...] = (a * jax.nn.sigmoid(a) * b).astype(o_ref.dtype)

    @pl.when(jnp.logical_not(used))
    def _():
        o_ref[...] = jnp.zeros_like(o_ref)


def _expert_down_kernel(te_ref, nv_ref, x_ref, w_ref, o_ref, wb_ref):
    j = pl.program_id(1)
    used = j < nv_ref[0]

    @pl.when(used & _new_expert(te_ref, j))
    def _():
        wb_ref[...] = w_ref[...].astype(BF16)

    @pl.when(used)
    def _():
        o_ref[...] = jnp.dot(x_ref[...], wb_ref[...], preferred_element_type=F32)

    @pl.when(jnp.logical_not(used))
    def _():
        o_ref[...] = jnp.zeros_like(o_ref)


def _expert_matmul(body, tile_expert, n_used, x, weights, layer, out_dtype, tn, tm, name):
    rows, k = x.shape
    n = weights[0].shape[-1]
    last = lambda j, nv: jnp.minimum(j, nv[0] - 1)
    wspec = pl.BlockSpec((None, None, k, tn), lambda c, j, te, nv: (layer, te[last(j, nv)], 0, c))
    grid_spec = pltpu.PrefetchScalarGridSpec(
        num_scalar_prefetch=2, grid=(n // tn, rows // tm),
        in_specs=[pl.BlockSpec((tm, k), lambda c, j, te, nv: (last(j, nv), 0))] + [wspec] * len(weights),
        out_specs=pl.BlockSpec((tm, tn), lambda c, j, te, nv: (j, c)),
        scratch_shapes=[pltpu.VMEM((k, tn), BF16)] * len(weights))
    return pl.pallas_call(
        body, grid_spec=grid_spec, out_shape=jax.ShapeDtypeStruct((rows, n), out_dtype),
        compiler_params=_params(("arbitrary", "arbitrary")), name=name)(tile_expert, n_used, x, *weights)


def _moe(h_bf16, h_f32, w_router, b_router, w_gate, w_up, w_down, j):
    m, d = h_f32.shape
    ne = w_router.shape[-1]
    tm_e = 2 * MXU_TILE
    idx, wgt, rank, counts = _router(h_bf16, w_router, b_router, j)
    padded = (counts[0] + tm_e - 1) // tm_e * tm_e
    ends = jnp.cumsum(padded)
    starts = ends - padded
    n_tiles = -(-(TOP_K * m + ne * (tm_e - 1)) // tm_e)
    pos_flat = (starts[idx] + rank).reshape(-1)
    tile_expert = jnp.minimum(
        jnp.searchsorted(ends, jnp.arange(n_tiles, dtype=jnp.int32) * tm_e, side="right"), ne - 1
    ).astype(jnp.int32)
    n_used = (ends[-1:] // tm_e).astype(jnp.int32)
    xs = _dispatch(pos_flat, h_f32, n_tiles * tm_e, MXU_TILE)
    act = _expert_matmul(_expert_up_kernel, tile_expert, n_used, xs, (w_gate, w_up), j, BF16,
                         _tile(w_gate.shape[-1], 512, V7X_LANES), tm_e, "expert_up")
    ys = _expert_matmul(_expert_down_kernel, tile_expert, n_used, act, (w_down,), j, F32,
                        _tile(d, 512, V7X_LANES), tm_e, "expert_down")
    return _combine(pos_flat, wgt, ys, _tile(m, 256, 8))


def _rope_tables(positions):
    half = QK_ROPE // 2
    inv = ROPE_BASE ** (-jnp.arange(half, dtype=F32) / half)
    ang = positions.astype(F32)[:, None] * inv
    cos, sin = jnp.cos(ang), jnp.sin(ang)
    reps = V7X_LANES // QK_ROPE
    return (jnp.tile(jnp.concatenate([cos, cos], axis=1), (1, reps)),
            jnp.tile(jnp.concatenate([-sin, sin], axis=1), (1, reps)))


def kernel(x_prompt, x_sample, cache_ckv, cache_krope, state_pool, state_lru_conv, state_lru_h, page_table, norm_mix_pre, norm_mix_post, norm_ffn_pre, norm_ffn_post, w_in, pool_w, pool_scale, lru_conv_w, lru_conv_b, lru_wa, lru_ba, lru_wx, lru_bx, lru_lambda, mla_q_norm, mla_w_uq, mla_kv_norm, mla_w_ukv, w_branch_pool, w_branch_lru, w_branch_mla, w_out, ffn_w_gate, ffn_w_up, ffn_w_down, moe_w_router, moe_b_router, moe_w_gate, moe_w_up, moe_w_down):
    batch, seq, d = x_prompt.shape
    nseq, dec_seq, _ = x_sample.shape
    assert dec_seq == 1
    depth = w_in.shape[0]
    pool_width = pool_scale.shape[1]
    lru_width = lru_lambda.shape[1]
    q_lora = mla_q_norm.shape[1]
    kv_lora = mla_kv_norm.shape[1]
    qk_head = QK_NOPE + QK_ROPE
    heads = mla_w_uq.shape[2] // qk_head
    past_len = page_table.shape[1] * cache_ckv.shape[2]
    scale = float(qk_head) ** -0.5
    mp = batch * seq
    m = mp + nseq
    o_lru_in = pool_width
    o_q = o_lru_in + lru_width
    o_kv = o_q + q_lora
    o_kr = o_kv + kv_lora
    o_gate = o_kr + QK_ROPE
    assert pool_width == lru_width == q_lora and o_kv % kv_lora == 0 and o_kr % V7X_LANES == 0
    assert mp % nseq == 0

    vec3 = lambda a: a.reshape(a.shape[0], 1, a.shape[1])
    pool_scale, lru_conv_b, lru_ba, lru_bx, lru_lambda, mla_q_norm, mla_kv_norm, moe_b_router = map(
        vec3, (pool_scale, lru_conv_b, lru_ba, lru_bx, lru_lambda, mla_q_norm, mla_kv_norm, moe_b_router))

    tm = _tile(m, 1040, V7X_BF16_SUBLANES)
    tm_small = _tile(m, 520, V7X_BF16_SUBLANES)
    tmp = _tile(mp, 1024, V7X_BF16_SUBLANES)
    tn = _tile(o_kr, 512, V7X_LANES)

    x = jnp.concatenate([x_prompt.reshape(mp, d), x_sample.reshape(nseq, d)], axis=0)
    positions = jnp.concatenate([jnp.tile(jnp.arange(seq), batch), jnp.full((nseq,), past_len)])
    cos, sin = _rope_tables(positions)
    w_in_t = jnp.swapaxes(w_in, 1, 2)
    cache_krope_t = jnp.swapaxes(cache_krope, 2, 3)

    outs_prompt = [[] for _ in range(5)]
    outs_sample = [[] for _ in range(5)]
    h = _norm(x, norm_mix_pre[0])
    for i in range(depth):
        z = _mm(h, w_in_t, prefix=(i,), ncols=o_kr, tn=tn, tm=tm, out_dtype=F32, name="in_proj",
                w_rows=True)
        z_kr = _mm(h, w_in_t, prefix=(i,), col0=o_kr, ncols=V7X_LANES, tn=V7X_LANES, tm=tm,
                   out_dtype=F32, name="in_proj_kr", w_rows=True)
        gate_logits = _mm(h, w_in_t, prefix=(i,), col0=o_gate, ncols=3 * d, tn=_tile(d, 512, V7X_LANES),
                          tm=tm, out_dtype=F32, name="in_proj_gates", w_rows=True)

        op_p = _pool_prompt(z, pool_w, pool_scale, i, batch, seq, pool_width)
        op_s = _pool_sample(z, state_pool[i].swapaxes(0, 1), pool_w, pool_scale, i, mp // nseq, nseq,
                            pool_width)
        o_pool = jnp.concatenate([op_p, op_s], axis=0)
        u_pool_p = z[:mp, :pool_width].reshape(batch, seq, pool_width)
        outs_prompt[2].append(u_pool_p[:, seq - state_pool.shape[2]:])
        outs_sample[2].append(jnp.concatenate([state_pool[i, :, 1:], z[mp:, None, :pool_width]], axis=1))

        chunk = min(MXU_TILE, lru_width)
        wa_bd = _block_diag(lru_wa[i], chunk)
        wx_bd = _block_diag(lru_wx[i], chunk)
        ol_p, hl_p = _lru_prompt(z, o_lru_in // lru_width, lru_conv_w, lru_conv_b, wa_bd, wx_bd,
                                 lru_ba, lru_bx, lru_lambda, i, batch, seq, lru_width)
        ol_s, hl_s = _lru_sample(z, mp // nseq, o_lru_in // lru_width, state_lru_conv[i].swapaxes(0, 1),
                                 state_lru_h, lru_conv_w, lru_conv_b, wa_bd, wx_bd, lru_ba, lru_bx,
                                 lru_lambda, i, nseq, lru_width)
        o_lru = jnp.concatenate([ol_p, ol_s], axis=0)
        u_lru = z[:, o_lru_in:o_q]
        outs_prompt[3].append(u_lru[:mp].reshape(batch, seq, lru_width)[:, seq - state_lru_conv.shape[2]:])
        outs_sample[3].append(jnp.concatenate([state_lru_conv[i, :, 1:], u_lru[mp:, None]], axis=1))
        outs_prompt[4].append(hl_p)
        outs_sample[4].append(hl_s)

        cq, ckv, ckv_b, kr, kr_b = _mla_prep(z, z_kr, cos, sin, mla_q_norm, mla_kv_norm, i,
                                             o_q // q_lora, o_kv // kv_lora, q_lora, kv_lora)
        outs_prompt[0].append(ckv[:mp].reshape(batch, seq, kv_lora))
        outs_prompt[1].append(kr[:mp].reshape(batch, seq, QK_ROPE))
        outs_sample[0].append(ckv[mp:].reshape(nseq, 1, kv_lora))
        outs_sample[1].append(kr[mp:].reshape(nseq, 1, QK_ROPE))
        w_uq = mla_w_uq[i].reshape(q_lora, heads, qk_head)
        w_uq_n = w_uq[:, :, :QK_NOPE].reshape(q_lora, heads * QK_NOPE)
        w_uq_r = w_uq[:, :, QK_NOPE:].reshape(q_lora, heads * QK_ROPE)
        qn = _mm(cq, w_uq_n, ncols=heads * QK_NOPE, tn=_tile(heads * QK_NOPE, 1024, V7X_LANES), tm=tm,
                 out_dtype=BF16, name="q_nope")
        qr = _mm(cq, w_uq_r, ncols=heads * QK_ROPE, tn=_tile(heads * QK_ROPE, 1024, V7X_LANES), tm=tm,
                 out_dtype=BF16, epilogue="rope", extra=(cos, sin), name="q_rope")
        w_kv = mla_w_ukv[i].reshape(kv_lora, heads, QK_NOPE + V_HEAD)
        w_k = w_kv[:, :, :QK_NOPE].reshape(kv_lora, heads * QK_NOPE)
        w_v = w_kv[:, :, QK_NOPE:].reshape(kv_lora, heads * V_HEAD)
        kn = _mm(ckv_b, w_k, ncols=heads * QK_NOPE, tn=_tile(heads * QK_NOPE, 1024, V7X_LANES), tm=tmp,
                 rows=mp, out_dtype=BF16, name="k_nope")
        vv = _mm(ckv_b, w_v, ncols=heads * V_HEAD, tn=_tile(heads * V_HEAD, 1024, V7X_LANES), tm=tmp,
                 rows=mp, out_dtype=BF16, name="v_up_prompt")
        om_p = _attn_prompt(qn, qr, kn, kr_b, vv, batch, seq, heads, scale)

        q_lat = _q_latent(qn, mla_w_ukv, i, mp // nseq, nseq, heads, kv_lora)
        o_lat = _paged_attention(page_table, q_lat.reshape(nseq, heads, kv_lora),
                                 qr[mp:].reshape(nseq, heads, QK_ROPE),
                                 ckv_b[mp:].reshape(nseq, 1, kv_lora), kr_b[mp:].reshape(nseq, 1, V7X_LANES),
                                 cache_ckv, cache_krope_t, i, scale)
        om_s = _v_up(o_lat.reshape(nseq, heads * kv_lora), mla_w_ukv, i, nseq, heads, kv_lora)
        o_mla = jnp.concatenate([om_p, om_s], axis=0)

        merged = _merge(o_pool, o_lru, o_mla, w_branch_pool, w_branch_lru, w_branch_mla, gate_logits, i,
                        tn=_tile(d, 512, V7X_LANES), tm=tm_small)
        mix = _mm(merged, w_out, prefix=(i,), ncols=d, tn=_tile(d, 512, V7X_LANES), tm=tm, out_dtype=F32,
                  name="out_proj")
        routed = i % 2 == 1
        x, h2, *h2_f32 = _resid_norm(x, mix, norm_mix_post[i], norm_ffn_pre[i], also_f32=routed)

        j = i // 2
        if not routed:
            d_ff = ffn_w_gate.shape[2]
            act = _swiglu_up(h2, ffn_w_gate, ffn_w_up, (j,), tn=_tile(d_ff, 256, V7X_LANES), tm=tm)
            halves = 2 if (d_ff // 2) % V7X_LANES == 0 else 1
            f = None
            for kb in range(halves):
                f = _mm(act, ffn_w_down, prefix=(j,), ncols=d, tn=_tile(d, 512, V7X_LANES),
                        tm=_tile(m, 640, V7X_BF16_SUBLANES), out_dtype=F32, name="ffn_down",
                        k_block=(kb, d_ff // halves), epilogue="none" if f is None else "add",
                        extra=() if f is None else (f,))
        else:
            f = _moe(h2, h2_f32[0], moe_w_router, moe_b_router, moe_w_gate, moe_w_up, moe_w_down, j)
        g_next = norm_mix_pre[i + 1] if i + 1 < depth else None
        x, h = _resid_norm(x, f, norm_ffn_post[i], g_next)

    y_prompt = x[:mp].reshape(batch, seq, d)
    y_sample = x[mp:].reshape(nseq, 1, d)
    st = lambda lst: jnp.stack(lst)
    return (y_prompt, y_sample, st(outs_prompt[0]), st(outs_prompt[1]), st(outs_prompt[2]),
            st(outs_prompt[3]), st(outs_prompt[4]), st(outs_sample[0]), st(outs_sample[1]),
            st(outs_sample[2]), st(outs_sample[3]), st(outs_sample[4]))
```

```python
import functools

import jax
import jax.numpy as jnp
from jax import lax
from jax.experimental import pallas as pl
from jax.experimental.pallas import tpu as pltpu

BF16 = jnp.bfloat16
F32 = jnp.float32

V7X_LANES = 128
V7X_BF16_SUBLANES = 16
V7X_VMEM_LIMIT_BYTES = 56 * 1024 * 1024

POOL_WINDOWS = (2, 4, 8, 16)
LRU_C = 8.0
QK_NOPE = 128
QK_ROPE = 64
V_HEAD = 128
ROPE_BASE = 10000.0
TOP_K = 2
EPS = 1e-6
MXU_TILE = 256


def _tile(n, target, mult):
    best = None
    for t in range(mult, min(n, target) + 1, mult):
        if n % t == 0:
            best = t
    return best if best is not None else n


def _params(sem):
    return pltpu.CompilerParams(dimension_semantics=sem, vmem_limit_bytes=V7X_VMEM_LIMIT_BYTES)


def _wspec(w, k, tn, prefix, col0, kb=0):
    nlead = len(prefix)
    return pl.BlockSpec((None,) * nlead + (k, tn), lambda n, m: tuple(prefix) + (kb, n + col0))


def _wspec_t(k, tn, prefix, row0):
    nlead = len(prefix)
    return pl.BlockSpec((pl.Element(1),) * nlead + (pl.Element(tn), pl.Element(k)),
                        lambda n, m: tuple(prefix) + (pl.multiple_of(row0 + n * tn, 8), 0))


def _rms(x, g):
    return x * lax.rsqrt(jnp.mean(x * x, axis=-1, keepdims=True) + EPS) * g


def _norm_kernel(x_ref, g_ref, h_ref):
    h_ref[...] = _rms(x_ref[...], g_ref[...]).astype(h_ref.dtype)


def _pack_bf16_pairs(h):
    half = h.shape[1] // 2
    bits = lax.bitcast_convert_type(h.astype(F32), jnp.uint32)
    return (bits[:, :half] >> 16) | (bits[:, half:] & jnp.uint32(0xFFFF0000))


def _unpack_bf16_pairs(p):
    lo = lax.bitcast_convert_type(p << 16, F32)
    hi = lax.bitcast_convert_type(p & jnp.uint32(0xFFFF0000), F32)
    return jnp.concatenate([lo, hi], axis=1).astype(BF16)


def _resid_norm_kernel(x_ref, f_ref, gp_ref, gn_ref, xo_ref, h_ref, *packed_ref):
    x = x_ref[...] + _rms(f_ref[...], gp_ref[...])
    xo_ref[...] = x
    h = _rms(x, gn_ref[...]).astype(h_ref.dtype)
    h_ref[...] = h
    for p_ref in packed_ref:
        p_ref[...] = _pack_bf16_pairs(h)


def _resid_kernel(x_ref, f_ref, gp_ref, xo_ref):
    xo_ref[...] = x_ref[...] + _rms(f_ref[...], gp_ref[...])


def _norm(x, g):
    m, d = x.shape
    tm = _tile(m, 256, V7X_BF16_SUBLANES)
    row = pl.BlockSpec((tm, d), lambda i: (i, 0))
    vec = pl.BlockSpec((1, d), lambda i: (0, 0))
    return pl.pallas_call(
        _norm_kernel, grid=(m // tm,), in_specs=[row, vec], out_specs=row,
        out_shape=jax.ShapeDtypeStruct((m, d), BF16), compiler_params=_params(("arbitrary",)),
        name="rms_norm")(x, g.reshape(1, d))


def _resid_norm(x, f, g_post, g_next, also_packed=False):
    m, d = x.shape
    tm = _tile(m, 256, V7X_BF16_SUBLANES)
    row = pl.BlockSpec((tm, d), lambda i: (i, 0))
    vec = pl.BlockSpec((1, d), lambda i: (0, 0))
    if g_next is None:
        return pl.pallas_call(
            _resid_kernel, grid=(m // tm,), in_specs=[row, row, vec], out_specs=row,
            out_shape=jax.ShapeDtypeStruct((m, d), F32), compiler_params=_params(("arbitrary",)),
            name="resid")(x, f, g_post.reshape(1, d)), None
    out_specs = [row, row]
    out_shape = [jax.ShapeDtypeStruct((m, d), F32), jax.ShapeDtypeStruct((m, d), BF16)]
    if also_packed:
        out_specs.append(pl.BlockSpec((tm, d // 2), lambda i: (i, 0)))
        out_shape.append(jax.ShapeDtypeStruct((m, d // 2), jnp.uint32))
    return pl.pallas_call(
        _resid_norm_kernel, grid=(m // tm,), in_specs=[row, row, vec, vec],
        out_specs=out_specs, out_shape=out_shape,
        compiler_params=_params(("arbitrary",)),
        name="resid_norm")(x, f, g_post.reshape(1, d), g_next.reshape(1, d))


def _rope_tile(acc, cos_ref, sin_ref):
    tn = acc.shape[1]
    reps = tn // V7X_LANES
    cos = jnp.tile(cos_ref[...], (1, reps))
    sin = jnp.tile(sin_ref[...], (1, reps))
    lane = lax.broadcasted_iota(jnp.int32, acc.shape, 1) % QK_ROPE
    half = QK_ROPE // 2
    partner = jnp.where(lane < half, pltpu.roll(acc, tn - half, 1), pltpu.roll(acc, half, 1))
    return acc * cos + partner * sin


_NT = (((1,), (1,)), ((), ()))


def _mm_kernel(*refs, epilogue, w_rows):
    if epilogue == "rope":
        x_ref, w_ref, cos_ref, sin_ref, o_ref, wb_ref = refs
    elif epilogue == "add":
        x_ref, w_ref, a_ref, o_ref, wb_ref = refs
    else:
        x_ref, w_ref, o_ref, wb_ref = refs

    @pl.when(pl.program_id(1) == 0)
    def _():
        wb_ref[...] = w_ref[(0,) * (len(w_ref.shape) - 2)].astype(BF16)

    if w_rows:
        acc = lax.dot_general(x_ref[...], wb_ref[...], _NT, preferred_element_type=F32)
    else:
        acc = jnp.dot(x_ref[...], wb_ref[...], preferred_element_type=F32)
    if epilogue == "rope":
        acc = _rope_tile(acc, cos_ref, sin_ref)
    elif epilogue == "add":
        acc = a_ref[...] + acc
    o_ref[...] = acc.astype(o_ref.dtype)


def _mm(x, w, *, prefix=(), col0=0, ncols, tn, tm, out_dtype, rows=None, row0=0,
        epilogue="none", extra=(), name="mm", w_rows=False, k_block=None):
    m, k = x.shape
    kb, k = (0, k) if k_block is None else k_block
    rows = m if rows is None else rows
    assert rows % tm == 0 and ncols % tn == 0
    grid = (ncols // tn, rows // tm)
    wspec = _wspec_t(k, tn, prefix, col0) if w_rows else _wspec(w, k, tn, prefix, col0, kb)
    in_specs = [pl.BlockSpec((tm, k), lambda n, i: (i + row0, kb)), wspec]
    if epilogue == "rope":
        in_specs += [pl.BlockSpec((tm, V7X_LANES), lambda n, i: (i + row0, 0))] * 2
    elif epilogue == "add":
        in_specs += [pl.BlockSpec((tm, tn), lambda n, i: (i, n))]
    return pl.pallas_call(
        functools.partial(_mm_kernel, epilogue=epilogue, w_rows=w_rows), grid=grid, in_specs=in_specs,
        out_specs=pl.BlockSpec((tm, tn), lambda n, i: (i, n)),
        out_shape=jax.ShapeDtypeStruct((rows, ncols), out_dtype),
        scratch_shapes=[pltpu.VMEM((tn, k) if w_rows else (k, tn), BF16)],
        compiler_params=_params(("arbitrary", "arbitrary")), name=name)(x, w, *extra)


def _swiglu_kernel(x_ref, wg_ref, wu_ref, o_ref, wgb_ref, wub_ref):
    @pl.when(pl.program_id(1) == 0)
    def _():
        wgb_ref[...] = wg_ref[...].astype(BF16)
        wub_ref[...] = wu_ref[...].astype(BF16)

    x = x_ref[...]
    a = jnp.dot(x, wgb_ref[...], preferred_element_type=F32)
    b = jnp.dot(x, wub_ref[...], preferred_element_type=F32)
    o_ref[...] = (a * jax.nn.sigmoid(a) * b).astype(o_ref.dtype)


def _swiglu_up(x, wg, wu, prefix, *, tn, tm):
    m, k = x.shape
    n = wg.shape[-1]
    return pl.pallas_call(
        _swiglu_kernel, grid=(n // tn, m // tm),
        in_specs=[pl.BlockSpec((tm, k), lambda j, i: (i, 0)),
                  _wspec(wg, k, tn, prefix, 0), _wspec(wu, k, tn, prefix, 0)],
        out_specs=pl.BlockSpec((tm, tn), lambda j, i: (i, j)),
        out_shape=jax.ShapeDtypeStruct((m, n), BF16),
        scratch_shapes=[pltpu.VMEM((k, tn), BF16), pltpu.VMEM((k, tn), BF16)],
        compiler_params=_params(("arbitrary", "arbitrary")), name="swiglu_up")(x, wg, wu)


def _merge_kernel(op_ref, ol_ref, om_ref, wp_ref, wl_ref, wm_ref, gp_ref, gl_ref, gm_ref,
                  o_ref, wpb_ref, wlb_ref, wmb_ref):
    @pl.when(pl.program_id(1) == 0)
    def _():
        wpb_ref[...] = wp_ref[...].astype(BF16)
        wlb_ref[...] = wl_ref[...].astype(BF16)
        wmb_ref[...] = wm_ref[...].astype(BF16)

    acc = jax.nn.sigmoid(gp_ref[...]) * jnp.dot(op_ref[...], wpb_ref[...], preferred_element_type=F32)
    acc += jax.nn.sigmoid(gl_ref[...]) * jnp.dot(ol_ref[...], wlb_ref[...], preferred_element_type=F32)
    acc += jax.nn.sigmoid(gm_ref[...]) * jnp.dot(om_ref[...], wmb_ref[...], preferred_element_type=F32)
    o_ref[...] = acc.astype(o_ref.dtype)


def _merge(o_pool, o_lru, o_mla, wbp, wbl, wbm, gate_logits, layer, *, tn, tm):
    m = o_pool.shape[0]
    d = wbp.shape[-1]
    nb = d // tn
    lhs = lambda a: pl.BlockSpec((tm, a.shape[1]), lambda n, i: (i, 0))
    wsp = lambda w: _wspec(w, w.shape[-2], tn, (layer,), 0)
    gate = lambda g: pl.BlockSpec((tm, tn), lambda n, i: (i, n + g * nb))
    return pl.pallas_call(
        _merge_kernel, grid=(nb, m // tm),
        in_specs=[lhs(o_pool), lhs(o_lru), lhs(o_mla), wsp(wbp), wsp(wbl), wsp(wbm),
                  gate(0), gate(1), gate(2)],
        out_specs=pl.BlockSpec((tm, tn), lambda n, i: (i, n)),
        out_shape=jax.ShapeDtypeStruct((m, d), BF16),
        scratch_shapes=[pltpu.VMEM((wbp.shape[-2], tn), BF16), pltpu.VMEM((wbl.shape[-2], tn), BF16),
                        pltpu.VMEM((wbm.shape[-2], tn), BF16)],
        compiler_params=_params(("arbitrary", "arbitrary")),
        name="gated_merge")(o_pool, o_lru, o_mla, wbp, wbl, wbm, gate_logits, gate_logits, gate_logits)


def _pool_project(sums, u, cnts, pw_ref, ps_ref, group):
    outs = []
    for g in range(len(POOL_WINDOWS)):
        sl = slice(g * group, (g + 1) * group)
        pooled = sums[g][:, sl] / cnts[g] - u[:, sl]
        outs.append(jnp.dot(pooled.astype(BF16), pw_ref[g].astype(BF16), preferred_element_type=F32))
    return jnp.concatenate(outs, axis=1) * ps_ref[...]


def _pool_prompt_kernel(u_ref, halo_ref, pw_ref, ps_ref, o_ref, *, tt, group):
    t = pl.program_id(1)
    halo_rows = halo_ref.shape[0]
    halo = jnp.where(t == 0, 0.0, halo_ref[...])
    u = u_ref[...]
    full = jnp.concatenate([halo, u], axis=0)
    sums = []
    s = full
    span = 1
    for win in POOL_WINDOWS:
        while span < win:
            s = s + pltpu.roll(s, span, 0)
            span *= 2
        sums.append(s[halo_rows:])
    pos = t * tt + lax.broadcasted_iota(jnp.int32, (tt, 1), 0)
    cnts = [jnp.minimum(pos + 1, win).astype(F32) for win in POOL_WINDOWS]
    o_ref[...] = _pool_project(sums, u, cnts, pw_ref, ps_ref, group).astype(o_ref.dtype)


def _pool_prompt(z, pool_w, pool_scale, layer, batch, seq, width):
    tt = _tile(seq, 256, V7X_BF16_SUBLANES)
    halo = 16
    nt = seq // tt
    group = width // len(POOL_WINDOWS)
    return pl.pallas_call(
        functools.partial(_pool_prompt_kernel, tt=tt, group=group), grid=(batch, nt),
        in_specs=[pl.BlockSpec((tt, width), lambda b, t: (b * nt + t, 0)),
                  pl.BlockSpec((halo, width),
                               lambda b, t: (jnp.maximum((b * nt + t) * (tt // halo) - 1, 0), 0)),
                  pl.BlockSpec((None,) + pool_w.shape[1:], lambda b, t: (layer, 0, 0, 0)),
                  pl.BlockSpec((None, 1, width), lambda b, t: (layer, 0, 0))],
        out_specs=pl.BlockSpec((tt, width), lambda b, t: (b * nt + t, 0)),
        out_shape=jax.ShapeDtypeStruct((batch * seq, width), BF16),
        compiler_params=_params(("arbitrary", "arbitrary")), name="pool_prompt")(z, z, pool_w, pool_scale)


def _pool_sample_kernel(u_ref, st_ref, pw_ref, ps_ref, o_ref, *, group):
    u = u_ref[...]
    nbuf = st_ref.shape[0]
    sums = []
    s = u
    taken = 0
    for win in POOL_WINDOWS:
        while taken < win - 1:
            taken += 1
            s = s + st_ref[nbuf - taken]
        sums.append(s)
    cnts = [float(win) for win in POOL_WINDOWS]
    o_ref[...] = _pool_project(sums, u, cnts, pw_ref, ps_ref, group).astype(o_ref.dtype)


def _pool_sample(z, state_t, pool_w, pool_scale, layer, row_block, nrows, width):
    group = width // len(POOL_WINDOWS)
    return pl.pallas_call(
        functools.partial(_pool_sample_kernel, group=group), grid=(1,),
        in_specs=[pl.BlockSpec((nrows, width), lambda i: (row_block, 0)),
                  pl.BlockSpec(state_t.shape, lambda i: (0, 0, 0)),
                  pl.BlockSpec((None,) + pool_w.shape[1:], lambda i: (layer, 0, 0, 0)),
                  pl.BlockSpec((None, 1, width), lambda i: (layer, 0, 0))],
        out_specs=pl.BlockSpec((nrows, width), lambda i: (0, 0)),
        out_shape=jax.ShapeDtypeStruct((nrows, width), BF16),
        compiler_params=_params(("arbitrary",)), name="pool_sample")(z, state_t, pool_w, pool_scale)


def _lru_gates(xc, wa_ref, wx_ref, ba_ref, bx_ref, lam_ref):
    nchunk, cw, _ = wa_ref.shape
    xb = xc.astype(BF16)
    ra, rx = [], []
    for c in range(nchunk):
        xs = xb[:, c * cw:(c + 1) * cw]
        ra.append(jnp.dot(xs, wa_ref[c].astype(BF16), preferred_element_type=F32))
        rx.append(jnp.dot(xs, wx_ref[c].astype(BF16), preferred_element_type=F32))
    r = jax.nn.sigmoid(jnp.concatenate(ra, axis=1) + ba_ref[...])
    gi = jax.nn.sigmoid(jnp.concatenate(rx, axis=1) + bx_ref[...])
    nl = -lam_ref[...]
    softplus = jnp.maximum(nl, 0.0) + jnp.log1p(jnp.exp(-jnp.abs(nl)))
    log_a = -LRU_C * r * softplus
    a = jnp.exp(log_a)
    b = jnp.sqrt(jnp.tanh(-log_a) * (a * a + 1.0)) * (gi * xc)
    return a, b


def _lru_prompt_kernel(u_ref, cw_ref, cb_ref, wa_ref, wx_ref, ba_ref, bx_ref, lam_ref,
                       o_ref, hl_ref, tail_ref, h_ref, *, tc):
    t = pl.program_id(1)
    keep = tail_ref.shape[0]

    @pl.when(t == 0)
    def _():
        tail_ref[...] = jnp.zeros_like(tail_ref)
        h_ref[...] = jnp.zeros_like(h_ref)

    u = u_ref[...]
    full = jnp.concatenate([tail_ref[...], u], axis=0)
    taps = cw_ref.shape[0]
    xc = cb_ref[...] + cw_ref[taps - 1:taps, :] * u
    for k in range(1, taps):
        xc = xc + cw_ref[taps - 1 - k:taps - k, :] * pltpu.roll(full, k, 0)[keep:]
    tail_ref[...] = u[tc - keep:]

    a, b = _lru_gates(xc, wa_ref, wx_ref, ba_ref, bx_ref, lam_ref)
    row = lax.broadcasted_iota(jnp.int32, (tc, 1), 0)
    d = 1
    while d < tc:
        valid = row >= d
        a_prev = jnp.where(valid, pltpu.roll(a, d, 0), 1.0)
        b_prev = jnp.where(valid, pltpu.roll(b, d, 0), 0.0)
        b = a * b_prev + b
        a = a * a_prev
        d *= 2
    hs = a * h_ref[...] + b
    h_last = hs[tc - 1:tc]
    h_ref[...] = h_last
    hl_ref[...] = h_last
    o_ref[...] = hs.astype(o_ref.dtype)


def _lru_prompt(z, col_block, conv_w, conv_b, wa_bd, wx_bd, ba, bx, lam, layer, batch, seq, width):
    tc = _tile(seq, 256, V7X_BF16_SUBLANES)
    nt = seq // tc
    vec = pl.BlockSpec((None, 1, width), lambda b, t: (layer, 0, 0))
    bd = pl.BlockSpec(wa_bd.shape, lambda b, t: (0, 0, 0))
    o, h_last = pl.pallas_call(
        functools.partial(_lru_prompt_kernel, tc=tc), grid=(batch, nt),
        in_specs=[pl.BlockSpec((tc, width), lambda b, t: (b * nt + t, col_block)),
                  pl.BlockSpec((None,) + conv_w.shape[1:], lambda b, t: (layer, 0, 0)),
                  vec, bd, bd, vec, vec, vec],
        out_specs=[pl.BlockSpec((tc, width), lambda b, t: (b * nt + t, 0)),
                   pl.BlockSpec((None, 1, width), lambda b, t: (b, 0, 0))],
        out_shape=[jax.ShapeDtypeStruct((batch * seq, width), BF16),
                   jax.ShapeDtypeStruct((batch, 1, width), F32)],
        scratch_shapes=[pltpu.VMEM((8, width), F32), pltpu.VMEM((1, width), F32)],
        compiler_params=_params(("arbitrary", "arbitrary")),
        name="lru_prompt")(z, conv_w, conv_b, wa_bd, wx_bd, ba, bx, lam)
    return o, h_last.reshape(batch, width)


def _lru_sample_kernel(u_ref, cs_ref, h0_ref, cw_ref, cb_ref, wa_ref, wx_ref, ba_ref, bx_ref, lam_ref,
                       o_ref, hn_ref):
    taps = cw_ref.shape[0]
    xc = cb_ref[...] + cw_ref[taps - 1:taps, :] * u_ref[...]
    for k in range(taps - 1):
        xc = xc + cw_ref[k:k + 1, :] * cs_ref[k]
    a, b = _lru_gates(xc, wa_ref, wx_ref, ba_ref, bx_ref, lam_ref)
    h = a * h0_ref[...] + b
    hn_ref[...] = h
    o_ref[...] = h.astype(o_ref.dtype)


def _lru_sample(z, row_block, col_block, conv_t, h0, conv_w, conv_b, wa_bd, wx_bd, ba, bx, lam,
                layer, nrows, width):
    vec = pl.BlockSpec((None, 1, width), lambda i: (layer, 0, 0))
    bd = pl.BlockSpec(wa_bd.shape, lambda i: (0, 0, 0))
    return pl.pallas_call(
        _lru_sample_kernel, grid=(1,),
        in_specs=[pl.BlockSpec((nrows, width), lambda i: (row_block, col_block)),
                  pl.BlockSpec(conv_t.shape, lambda i: (0, 0, 0)),
                  pl.BlockSpec((None, nrows, width), lambda i: (layer, 0, 0)),
                  pl.BlockSpec((None,) + conv_w.shape[1:], lambda i: (layer, 0, 0)),
                  vec, bd, bd, vec, vec, vec],
        out_specs=[pl.BlockSpec((nrows, width), lambda i: (0, 0))] * 2,
        out_shape=[jax.ShapeDtypeStruct((nrows, width), BF16), jax.ShapeDtypeStruct((nrows, width), F32)],
        compiler_params=_params(("arbitrary",)),
        name="lru_sample")(z, conv_t, h0, conv_w, conv_b, wa_bd, wx_bd, ba, bx, lam)


def _block_diag(w, chunk):
    heads, blk, _ = w.shape
    per = chunk // blk
    eye = jnp.eye(per, dtype=w.dtype)
    wr = w.reshape(heads // per, per, blk, blk)
    return jnp.einsum("cpij,pq->cpiqj", wr, eye).reshape(heads // per, chunk, chunk)


def _mla_prep_kernel(cq_ref, ckv_ref, kr_ref, cos_ref, sin_ref, gq_ref, gkv_ref,
                     cqn_ref, ckvn_ref, ckvb_ref, krn_ref, krb_ref):
    cqn_ref[...] = _rms(cq_ref[...], gq_ref[...]).astype(cqn_ref.dtype)
    ckv = _rms(ckv_ref[...], gkv_ref[...])
    ckvn_ref[...] = ckv
    ckvb_ref[...] = ckv.astype(ckvb_ref.dtype)
    x = kr_ref[...]
    lane = lax.broadcasted_iota(jnp.int32, x.shape, 1)
    half = QK_ROPE // 2
    partner = jnp.where(lane % QK_ROPE < half, pltpu.roll(x, V7X_LANES - half, 1), pltpu.roll(x, half, 1))
    kr = jnp.where(lane < QK_ROPE, x * cos_ref[...] + partner * sin_ref[...], 0.0)
    krn_ref[...] = kr[:, :QK_ROPE]
    krb_ref[...] = (kr + pltpu.roll(kr, QK_ROPE, 1)).astype(krb_ref.dtype)


def _mla_prep(z, z_kr, cos, sin, q_norm, kv_norm, layer, q_block, kv_block, q_lora, kv_lora):
    m = z.shape[0]
    tm = _tile(m, 512, V7X_BF16_SUBLANES)
    rows = lambda w, c: pl.BlockSpec((tm, w), lambda i: (i, c))
    return pl.pallas_call(
        _mla_prep_kernel, grid=(m // tm,),
        in_specs=[rows(q_lora, q_block), rows(kv_lora, kv_block), rows(V7X_LANES, 0),
                  rows(V7X_LANES, 0), rows(V7X_LANES, 0),
                  pl.BlockSpec((None, 1, q_lora), lambda i: (layer, 0, 0)),
                  pl.BlockSpec((None, 1, kv_lora), lambda i: (layer, 0, 0))],
        out_specs=[rows(q_lora, 0), rows(kv_lora, 0), rows(kv_lora, 0), rows(QK_ROPE, 0),
                   rows(V7X_LANES, 0)],
        out_shape=[jax.ShapeDtypeStruct((m, q_lora), BF16), jax.ShapeDtypeStruct((m, kv_lora), F32),
                   jax.ShapeDtypeStruct((m, kv_lora), BF16), jax.ShapeDtypeStruct((m, QK_ROPE), F32),
                   jax.ShapeDtypeStruct((m, V7X_LANES), BF16)],
        compiler_params=_params(("arbitrary",)), name="mla_prep")(z, z, z_kr, cos, sin, q_norm, kv_norm)


ATTN_HEADS_PER_STEP = 4


def _attn_prompt_kernel(qn_ref, qr_ref, kn_ref, kr_ref, v_ref, o_ref, *, tq, tk, scale):
    qi = pl.program_id(2)
    lane = lax.broadcasted_iota(jnp.int32, (tq, V7X_LANES), 1)
    qpos = qi * tq + lax.broadcasted_iota(jnp.int32, (tq, tk), 0)
    kidx = lax.broadcasted_iota(jnp.int32, (tq, tk), 1)
    queries = []
    for hh in range(ATTN_HEADS_PER_STEP):
        pair = qr_ref[:, (hh // 2) * V7X_LANES:(hh // 2 + 1) * V7X_LANES]
        in_head = (lane >= (hh % 2) * QK_ROPE) & (lane < (hh % 2 + 1) * QK_ROPE)
        qr = jnp.where(in_head, pair, jnp.zeros_like(pair))
        queries.append(jnp.concatenate([qn_ref[:, hh * QK_NOPE:(hh + 1) * QK_NOPE], qr], axis=1))

    def body(kb, carry):
        off = pl.multiple_of(kb * tk, tk)
        kr = kr_ref[pl.ds(off, tk), :]
        visible = kidx + off <= qpos
        new = []
        for hh, (m, l, acc) in enumerate(carry):
            hs = slice(hh * QK_NOPE, (hh + 1) * QK_NOPE)
            keys = jnp.concatenate([kn_ref[pl.ds(off, tk), hs], kr], axis=1)
            s = lax.dot_general(queries[hh], keys, _NT, preferred_element_type=F32) * scale
            s = jnp.where(visible, s, -jnp.inf)
            m_new = jnp.maximum(m, jnp.max(s, axis=1, keepdims=True))
            corr = jnp.exp(m - m_new)
            p = jnp.exp(s - m_new)
            l = l * corr + jnp.sum(p, axis=1, keepdims=True)
            acc = acc * corr + jnp.dot(p.astype(BF16), v_ref[pl.ds(off, tk), hs], preferred_element_type=F32)
            new.append((m_new, l, acc))
        return tuple(new)

    init = tuple((jnp.full((tq, 1), -jnp.inf, F32), jnp.zeros((tq, 1), F32), jnp.zeros((tq, V_HEAD), F32))
                 for _ in range(ATTN_HEADS_PER_STEP))
    nkb = (qi * tq + tq + tk - 1) // tk
    final = lax.fori_loop(0, nkb, body, init)
    o_ref[...] = jnp.concatenate([acc / l for _, l, acc in final], axis=1).astype(o_ref.dtype)


def _attn_prompt(qn, qr, kn, krb, v, batch, seq, heads, scale):
    tq = _tile(seq, 256, V7X_BF16_SUBLANES)
    tk = _tile(seq, 512, tq)
    nq = seq // tq
    hps = ATTN_HEADS_PER_STEP
    wide = hps * QK_NOPE
    return pl.pallas_call(
        functools.partial(_attn_prompt_kernel, tq=tq, tk=tk, scale=scale), grid=(batch, heads // hps, nq),
        in_specs=[pl.BlockSpec((tq, wide), lambda b, h, q: (b * nq + q, h)),
                  pl.BlockSpec((tq, hps * QK_ROPE), lambda b, h, q: (b * nq + q, h)),
                  pl.BlockSpec((seq, wide), lambda b, h, q: (b, h)),
                  pl.BlockSpec((seq, V7X_LANES), lambda b, h, q: (b, 0)),
                  pl.BlockSpec((seq, wide), lambda b, h, q: (b, h))],
        out_specs=pl.BlockSpec((tq, wide), lambda b, h, q: (b * nq + q, h)),
        out_shape=jax.ShapeDtypeStruct((batch * seq, heads * V_HEAD), BF16),
        compiler_params=_params(("arbitrary", "arbitrary", "arbitrary")),
        name="attn_prompt")(qn, qr, kn, krb, v)


def _head_map_kernel(x_ref, w_ref, o_ref, *, transpose_w):
    w = w_ref[...].astype(BF16)
    dims = (((1,), (1,)), ((), ())) if transpose_w else (((1,), (0,)), ((), ()))
    o_ref[...] = lax.dot_general(x_ref[...], w, dims, preferred_element_type=F32).astype(o_ref.dtype)


def _q_latent(qn, w_ukv, layer, row_block, nrows, heads, kv_lora):
    return pl.pallas_call(
        functools.partial(_head_map_kernel, transpose_w=True), grid=(heads,),
        in_specs=[pl.BlockSpec((nrows, QK_NOPE), lambda h: (row_block, h)),
                  pl.BlockSpec((None, kv_lora, QK_NOPE), lambda h: (layer, 0, 2 * h))],
        out_specs=pl.BlockSpec((nrows, kv_lora), lambda h: (0, h)),
        out_shape=jax.ShapeDtypeStruct((nrows, heads * kv_lora), BF16),
        compiler_params=_params(("arbitrary",)), name="q_latent")(qn, w_ukv)


def _v_up(o_lat, w_ukv, layer, nrows, heads, kv_lora):
    return pl.pallas_call(
        functools.partial(_head_map_kernel, transpose_w=False), grid=(heads,),
        in_specs=[pl.BlockSpec((nrows, kv_lora), lambda h: (0, h)),
                  pl.BlockSpec((None, kv_lora, V_HEAD), lambda h: (layer, 0, 2 * h + 1))],
        out_specs=pl.BlockSpec((nrows, V_HEAD), lambda h: (0, h)),
        out_shape=jax.ShapeDtypeStruct((nrows, heads * V_HEAD), BF16),
        compiler_params=_params(("arbitrary",)), name="v_up")(o_lat, w_ukv)


PAGED_SLOTS = 3


def _paged_kernel(pt_ref, ql_ref, qr_ref, cs_ref, ks_ref, ck_hbm, kp_hbm, o_ref, ck_buf, kp_buf, sem,
                  *, layer, group, n_groups, n_seq, scale):
    b = pl.program_id(0)
    total = n_seq * n_groups
    n_slots, _, page, kv_lora = ck_buf.shape
    ahead = n_slots - 1

    def group_copies(k):
        seq = k // n_groups
        first = (k % n_groups) * group
        slot = k % n_slots
        copies = []
        for j in range(group):
            phys = pt_ref[seq, first + j]
            copies.append(pltpu.make_async_copy(ck_hbm.at[layer, phys], ck_buf.at[slot, j], sem.at[slot, 0]))
            copies.append(pltpu.make_async_copy(kp_hbm.at[layer, phys], kp_buf.at[slot, j], sem.at[slot, 1]))
        return copies

    @pl.when(b == 0)
    def _():
        for k in range(min(ahead, total)):
            for c in group_copies(k):
                c.start()

    ql = ql_ref[...]
    qr = qr_ref[...]

    def body(g, carry):
        m, l, acc = carry
        k = b * n_groups + g
        slot = k % n_slots

        @pl.when(k + ahead < total)
        def _():
            for c in group_copies(k + ahead):
                c.start()

        for c in group_copies(k):
            c.wait()

        ck = ck_buf[slot].reshape(group * page, kv_lora).astype(BF16)
        s_rope = jnp.concatenate(
            [jnp.dot(qr, kp_buf[slot, j].astype(BF16), preferred_element_type=F32) for j in range(group)],
            axis=1)
        s = (lax.dot_general(ql, ck, _NT, preferred_element_type=F32) + s_rope) * scale
        m_new = jnp.maximum(m, jnp.max(s, axis=1, keepdims=True))
        corr = jnp.exp(m - m_new)
        pr = jnp.exp(s - m_new)
        l = l * corr + jnp.sum(pr, axis=1, keepdims=True)
        acc = acc * corr + jnp.dot(pr.astype(BF16), ck, preferred_element_type=F32)
        return m_new, l, acc

    heads = ql.shape[0]
    init = (jnp.full((heads, 1), -jnp.inf, F32), jnp.zeros((heads, 1), F32),
            jnp.zeros((heads, kv_lora), F32))
    m, l, acc = lax.fori_loop(0, n_groups, body, init)

    cs = cs_ref[...].astype(F32)
    ks = ks_ref[...][:, :QK_ROPE].astype(F32)
    s = (jnp.sum(ql.astype(F32) * cs, axis=1, keepdims=True)
         + jnp.sum(qr.astype(F32) * ks, axis=1, keepdims=True)) * scale
    m_new = jnp.maximum(m, s)
    corr = jnp.exp(m - m_new)
    pr = jnp.exp(s - m_new)
    l_fin = l * corr + pr
    acc_fin = acc * corr + pr.astype(BF16).astype(F32) * cs
    o_ref[...] = (acc_fin / l_fin).astype(o_ref.dtype)


def _paged_attention(page_table, q_lat, q_rope, ckv_self, kr_self, cache_ckv, cache_krope_t, layer, scale):
    nseq, heads, kv_lora = q_lat.shape
    n_pages = page_table.shape[1]
    page = cache_ckv.shape[2]
    group = _tile(n_pages, 16, 1)
    n_groups = n_pages // group
    seq_spec = lambda a: pl.BlockSpec((None,) + a.shape[1:], lambda b, pt: (b, 0, 0))
    grid_spec = pltpu.PrefetchScalarGridSpec(
        num_scalar_prefetch=1, grid=(nseq,),
        in_specs=[seq_spec(q_lat), seq_spec(q_rope), seq_spec(ckv_self), seq_spec(kr_self),
                  pl.BlockSpec(memory_space=pl.ANY), pl.BlockSpec(memory_space=pl.ANY)],
        out_specs=pl.BlockSpec((None, heads, kv_lora), lambda b, pt: (b, 0, 0)),
        scratch_shapes=[pltpu.VMEM((PAGED_SLOTS, group, page, kv_lora), F32),
                        pltpu.VMEM((PAGED_SLOTS, group, QK_ROPE, page), F32),
                        pltpu.SemaphoreType.DMA((PAGED_SLOTS, 2))])
    return pl.pallas_call(
        functools.partial(_paged_kernel, layer=layer, group=group, n_groups=n_groups, n_seq=nseq,
                          scale=scale),
        grid_spec=grid_spec,
        out_shape=jax.ShapeDtypeStruct((nseq, heads, kv_lora), BF16),
        compiler_params=_params(("arbitrary",)),
        name="paged_attention")(page_table, q_lat, q_rope, ckv_self, kr_self, cache_ckv, cache_krope_t)


def _router_kernel(x_ref, w_ref, b_ref, idx_ref, wgt_ref, rank_ref, cnt_ref, carry_ref):
    @pl.when(pl.program_id(0) == 0)
    def _():
        carry_ref[...] = jnp.zeros_like(carry_ref)

    logits = jnp.dot(x_ref[...], w_ref[...].astype(BF16), preferred_element_type=F32) + b_ref[...]
    tm, ne = logits.shape
    idx = lax.broadcasted_iota(jnp.int32, logits.shape, 1)
    v1 = jnp.max(logits, axis=1, keepdims=True)
    i1 = jnp.min(jnp.where(logits == v1, idx, ne), axis=1, keepdims=True)
    rest = jnp.where(idx == i1, -jnp.inf, logits)
    v2 = jnp.max(rest, axis=1, keepdims=True)
    i2 = jnp.min(jnp.where(rest == v2, idx, ne), axis=1, keepdims=True)
    e2 = jnp.exp(v2 - v1)
    denom = 1.0 + e2

    sel = ((idx == i1) | (idx == i2)).astype(F32)
    earlier = (lax.broadcasted_iota(jnp.int32, (tm, tm), 1)
               < lax.broadcasted_iota(jnp.int32, (tm, tm), 0)).astype(BF16)
    rank_all = jnp.dot(earlier, sel.astype(BF16), preferred_element_type=F32) + carry_ref[...]
    rank1 = jnp.sum(jnp.where(idx == i1, rank_all, 0.0), axis=1, keepdims=True)
    rank2 = jnp.sum(jnp.where(idx == i2, rank_all, 0.0), axis=1, keepdims=True)
    carry = carry_ref[...] + jnp.sum(sel, axis=0, keepdims=True)
    carry_ref[...] = carry
    cnt_ref[...] = carry.astype(jnp.int32)

    first = lax.broadcasted_iota(jnp.int32, (tm, TOP_K), 1) == 0
    idx_ref[...] = jnp.where(first, i1, i2)
    wgt_ref[...] = jnp.where(first, 1.0 / denom, e2 / denom)
    rank_ref[...] = jnp.where(first, rank1, rank2).astype(jnp.int32)


def _router(x, w_router, b_router, j):
    m, d = x.shape
    ne = w_router.shape[-1]
    tm = _tile(m, 520, V7X_BF16_SUBLANES)
    pair = pl.BlockSpec((tm, TOP_K), lambda i: (i, 0))
    return pl.pallas_call(
        _router_kernel, grid=(m // tm,),
        in_specs=[pl.BlockSpec((tm, d), lambda i: (i, 0)),
                  pl.BlockSpec((None, d, ne), lambda i: (j, 0, 0)),
                  pl.BlockSpec((None, 1, ne), lambda i: (j, 0, 0))],
        out_specs=[pair, pair, pair, pl.BlockSpec((1, ne), lambda i: (0, 0))],
        out_shape=[jax.ShapeDtypeStruct((m, TOP_K), jnp.int32), jax.ShapeDtypeStruct((m, TOP_K), F32),
                   jax.ShapeDtypeStruct((m, TOP_K), jnp.int32), jax.ShapeDtypeStruct((1, ne), jnp.int32)],
        scratch_shapes=[pltpu.VMEM((1, ne), F32)],
        compiler_params=_params(("arbitrary",)), name="router_top2")(x, w_router, b_router)


def _row_copy(src_hbm, src_row, dst, dst_row, sem):
    return pltpu.make_async_copy(src_hbm.at[pl.ds(src_row, 1)], dst.at[pl.ds(dst_row, 1)], sem)


def _dispatch_kernel(pos_ref, x_hbm, o_ref, inv_ref, buf, sem, *, n_tokens, tg):
    j = pl.program_id(0)
    nj = pl.num_programs(0)

    def start_tile(tile, slot):
        def body(r, c):
            _row_copy(x_hbm, inv_ref[tile * tg + r], buf.at[slot], r, sem.at[slot]).start()
            return c
        lax.fori_loop(0, tg, body, 0)

    @pl.when(j == 0)
    def _():
        def clear(p, c):
            inv_ref[p] = 0
            return c
        lax.fori_loop(0, nj * tg, clear, 0)

        def fill(t, c):
            for k in range(TOP_K):
                inv_ref[pos_ref[TOP_K * t + k]] = t
            return c
        lax.fori_loop(0, n_tokens, fill, 0)
        start_tile(0, 0)

    slot = j % 2

    @pl.when(j + 1 < nj)
    def _():
        start_tile(j + 1, 1 - slot)

    def wait(r, c):
        _row_copy(x_hbm, 0, buf.at[slot], r, sem.at[slot]).wait()
        return c
    lax.fori_loop(0, tg, wait, 0)
    o_ref[...] = _unpack_bf16_pairs(buf[slot])


def _dispatch(pos_flat, x_packed, n_slots, tg):
    m, half = x_packed.shape
    grid_spec = pltpu.PrefetchScalarGridSpec(
        num_scalar_prefetch=1, grid=(n_slots // tg,),
        in_specs=[pl.BlockSpec(memory_space=pl.ANY)],
        out_specs=pl.BlockSpec((tg, 2 * half), lambda j, pos: (j, 0)),
        scratch_shapes=[pltpu.SMEM((n_slots,), jnp.int32), pltpu.VMEM((2, tg, half), jnp.uint32),
                        pltpu.SemaphoreType.DMA((2,))])
    return pl.pallas_call(
        functools.partial(_dispatch_kernel, n_tokens=m, tg=tg), grid_spec=grid_spec,
        out_shape=jax.ShapeDtypeStruct((n_slots, 2 * half), BF16),
        compiler_params=_params(("arbitrary",)), name="moe_dispatch")(pos_flat, x_packed)


def _combine_kernel(pos_ref, w_ref, y_hbm, o_ref, buf, sem, *, tc):
    i = pl.program_id(0)
    ni = pl.num_programs(0)

    def start_tile(tile, slot):
        def body(r, c):
            for k in range(TOP_K):
                _row_copy(y_hbm, pos_ref[TOP_K * (tile * tc + r) + k], buf.at[slot, k], r, sem.at[slot]).start()
            return c
        lax.fori_loop(0, tc, body, 0)

    @pl.when(i == 0)
    def _():
        start_tile(0, 0)

    slot = i % 2

    @pl.when(i + 1 < ni)
    def _():
        start_tile(i + 1, 1 - slot)

    def wait(r, c):
        for k in range(TOP_K):
            _row_copy(y_hbm, 0, buf.at[slot, k], r, sem.at[slot]).wait()
        return c
    lax.fori_loop(0, tc, wait, 0)
    w = w_ref[...]
    o_ref[...] = w[:, 0:1] * buf[slot, 0] + w[:, 1:2] * buf[slot, 1]


def _combine(pos_flat, weights, y, tc):
    m = weights.shape[0]
    d = y.shape[1]
    grid_spec = pltpu.PrefetchScalarGridSpec(
        num_scalar_prefetch=1, grid=(m // tc,),
        in_specs=[pl.BlockSpec((tc, TOP_K), lambda i, pos: (i, 0)), pl.BlockSpec(memory_space=pl.ANY)],
        out_specs=pl.BlockSpec((tc, d), lambda i, pos: (i, 0)),
        scratch_shapes=[pltpu.VMEM((2, TOP_K, tc, d), F32), pltpu.SemaphoreType.DMA((2,))])
    return pl.pallas_call(
        functools.partial(_combine_kernel, tc=tc), grid_spec=grid_spec,
        out_shape=jax.ShapeDtypeStruct((m, d), F32),
        compiler_params=_params(("arbitrary",)), name="moe_combine")(pos_flat, weights, y)


def _new_expert(te_ref, j):
    return (j == 0) | (te_ref[j] != te_ref[jnp.maximum(j - 1, 0)])


def _expert_up_kernel(te_ref, nv_ref, x_ref, wg_ref, wu_ref, o_ref, wgb_ref, wub_ref):
    j = pl.program_id(1)
    used = j < nv_ref[0]

    @pl.when(used & _new_expert(te_ref, j))
    def _():
        wgb_ref[...] = wg_ref[...].astype(BF16)
        wub_ref[...] = wu_ref[...].astype(BF16)

    @pl.when(used)
    def _():
        x = x_ref[...]
        a = jnp.dot(x, wgb_ref[...], preferred_element_type=F32)
        b = jnp.dot(x, wub_ref[...], preferred_element_type=F32)
        o_ref[...] = (a * jax.nn.sigmoid(a) * b).astype(o_ref.dtype)

    @pl.when(jnp.logical_not(used))
    def _():
        o_ref[...] = jnp.zeros_like(o_ref)


def _expert_down_kernel(te_ref, nv_ref, x_ref, w_ref, o_ref, wb_ref):
    j = pl.program_id(1)
    used = j < nv_ref[0]

    @pl.when(used & _new_expert(te_ref, j))
    def _():
        wb_ref[...] = w_ref[...].astype(BF16)

    @pl.when(used)
    def _():
        o_ref[...] = jnp.dot(x_ref[...], wb_ref[...], preferred_element_type=F32)

    @pl.when(jnp.logical_not(used))
    def _():
        o_ref[...] = jnp.zeros_like(o_ref)


def _expert_matmul(body, tile_expert, n_used, x, weights, layer, out_dtype, tn, tm, name):
    rows, k = x.shape
    n = weights[0].shape[-1]
    last = lambda j, nv: jnp.minimum(j, nv[0] - 1)
    wspec = pl.BlockSpec((None, None, k, tn), lambda c, j, te, nv: (layer, te[last(j, nv)], 0, c))
    grid_spec = pltpu.PrefetchScalarGridSpec(
        num_scalar_prefetch=2, grid=(n // tn, rows // tm),
        in_specs=[pl.BlockSpec((tm, k), lambda c, j, te, nv: (last(j, nv), 0))] + [wspec] * len(weights),
        out_specs=pl.BlockSpec((tm, tn), lambda c, j, te, nv: (j, c)),
        scratch_shapes=[pltpu.VMEM((k, tn), BF16)] * len(weights))
    return pl.pallas_call(
        body, grid_spec=grid_spec, out_shape=jax.ShapeDtypeStruct((rows, n), out_dtype),
        compiler_params=_params(("arbitrary", "arbitrary")), name=name)(tile_expert, n_used, x, *weights)


def _moe(h_bf16, h_packed, w_router, b_router, w_gate, w_up, w_down, j):
    m, d = h_bf16.shape
    ne = w_router.shape[-1]
    tm_e = 2 * MXU_TILE
    idx, wgt, rank, counts = _router(h_bf16, w_router, b_router, j)
    padded = (counts[0] + tm_e - 1) // tm_e * tm_e
    ends = jnp.cumsum(padded)
    starts = ends - padded
    n_tiles = -(-(TOP_K * m + ne * (tm_e - 1)) // tm_e)
    pos_flat = (starts[idx] + rank).reshape(-1)
    tile_expert = jnp.minimum(
        jnp.searchsorted(ends, jnp.arange(n_tiles, dtype=jnp.int32) * tm_e, side="right"), ne - 1
    ).astype(jnp.int32)
    n_used = (ends[-1:] // tm_e).astype(jnp.int32)
    xs = _dispatch(pos_flat, h_packed, n_tiles * tm_e, MXU_TILE)
    act = _expert_matmul(_expert_up_kernel, tile_expert, n_used, xs, (w_gate, w_up), j, BF16,
                         _tile(w_gate.shape[-1], 512, V7X_LANES), tm_e, "expert_up")
    ys = _expert_matmul(_expert_down_kernel, tile_expert, n_used, act, (w_down,), j, F32,
                        _tile(d, 512, V7X_LANES), tm_e, "expert_down")
    return _combine(pos_flat, wgt, ys, _tile(m, 256, 8))


def _rope_tables(positions):
    half = QK_ROPE // 2
    inv = ROPE_BASE ** (-jnp.arange(half, dtype=F32) / half)
    ang = positions.astype(F32)[:, None] * inv
    cos, sin = jnp.cos(ang), jnp.sin(ang)
    reps = V7X_LANES // QK_ROPE
    return (jnp.tile(jnp.concatenate([cos, cos], axis=1), (1, reps)),
            jnp.tile(jnp.concatenate([-sin, sin], axis=1), (1, reps)))


def kernel(x_prompt, x_sample, cache_ckv, cache_krope, state_pool, state_lru_conv, state_lru_h, page_table, norm_mix_pre, norm_mix_post, norm_ffn_pre, norm_ffn_post, w_in, pool_w, pool_scale, lru_conv_w, lru_conv_b, lru_wa, lru_ba, lru_wx, lru_bx, lru_lambda, mla_q_norm, mla_w_uq, mla_kv_norm, mla_w_ukv, w_branch_pool, w_branch_lru, w_branch_mla, w_out, ffn_w_gate, ffn_w_up, ffn_w_down, moe_w_router, moe_b_router, moe_w_gate, moe_w_up, moe_w_down):
    batch, seq, d = x_prompt.shape
    nseq, dec_seq, _ = x_sample.shape
    assert dec_seq == 1
    depth = w_in.shape[0]
    pool_width = pool_scale.shape[1]
    lru_width = lru_lambda.shape[1]
    q_lora = mla_q_norm.shape[1]
    kv_lora = mla_kv_norm.shape[1]
    qk_head = QK_NOPE + QK_ROPE
    heads = mla_w_uq.shape[2] // qk_head
    past_len = page_table.shape[1] * cache_ckv.shape[2]
    scale = float(qk_head) ** -0.5
    mp = batch * seq
    m = mp + nseq
    o_lru_in = pool_width
    o_q = o_lru_in + lru_width
    o_kv = o_q + q_lora
    o_kr = o_kv + kv_lora
    o_gate = o_kr + QK_ROPE
    assert pool_width == lru_width == q_lora and o_kv % kv_lora == 0 and o_kr % V7X_LANES == 0
    assert mp % nseq == 0

    vec3 = lambda a: a.reshape(a.shape[0], 1, a.shape[1])
    pool_scale, lru_conv_b, lru_ba, lru_bx, lru_lambda, mla_q_norm, mla_kv_norm, moe_b_router = map(
        vec3, (pool_scale, lru_conv_b, lru_ba, lru_bx, lru_lambda, mla_q_norm, mla_kv_norm, moe_b_router))

    tm = _tile(m, 1040, V7X_BF16_SUBLANES)
    tm_small = _tile(m, 520, V7X_BF16_SUBLANES)
    tmp = _tile(mp, 1024, V7X_BF16_SUBLANES)
    tn = _tile(o_kr, 512, V7X_LANES)

    x = jnp.concatenate([x_prompt.reshape(mp, d), x_sample.reshape(nseq, d)], axis=0)
    positions = jnp.concatenate([jnp.tile(jnp.arange(seq), batch), jnp.full((nseq,), past_len)])
    cos, sin = _rope_tables(positions)
    w_in_t = jnp.swapaxes(w_in, 1, 2)
    cache_krope_t = jnp.swapaxes(cache_krope, 2, 3)

    outs_prompt = [[] for _ in range(5)]
    outs_sample = [[] for _ in range(5)]
    h = _norm(x, norm_mix_pre[0])
    for i in range(depth):
        z = _mm(h, w_in_t, prefix=(i,), ncols=o_kr, tn=tn, tm=tm, out_dtype=F32, name="in_proj",
                w_rows=True)
        z_kr = _mm(h, w_in_t, prefix=(i,), col0=o_kr, ncols=V7X_LANES, tn=V7X_LANES, tm=tm,
                   out_dtype=F32, name="in_proj_kr", w_rows=True)
        gate_logits = _mm(h, w_in_t, prefix=(i,), col0=o_gate, ncols=3 * d, tn=_tile(d, 512, V7X_LANES),
                          tm=tm, out_dtype=F32, name="in_proj_gates", w_rows=True)

        op_p = _pool_prompt(z, pool_w, pool_scale, i, batch, seq, pool_width)
        op_s = _pool_sample(z, state_pool[i].swapaxes(0, 1), pool_w, pool_scale, i, mp // nseq, nseq,
                            pool_width)
        o_pool = jnp.concatenate([op_p, op_s], axis=0)
        u_pool_p = z[:mp, :pool_width].reshape(batch, seq, pool_width)
        outs_prompt[2].append(u_pool_p[:, seq - state_pool.shape[2]:])
        outs_sample[2].append(jnp.concatenate([state_pool[i, :, 1:], z[mp:, None, :pool_width]], axis=1))

        chunk = min(MXU_TILE, lru_width)
        wa_bd = _block_diag(lru_wa[i], chunk)
        wx_bd = _block_diag(lru_wx[i], chunk)
        ol_p, hl_p = _lru_prompt(z, o_lru_in // lru_width, lru_conv_w, lru_conv_b, wa_bd, wx_bd,
                                 lru_ba, lru_bx, lru_lambda, i, batch, seq, lru_width)
        ol_s, hl_s = _lru_sample(z, mp // nseq, o_lru_in // lru_width, state_lru_conv[i].swapaxes(0, 1),
                                 state_lru_h, lru_conv_w, lru_conv_b, wa_bd, wx_bd, lru_ba, lru_bx,
                                 lru_lambda, i, nseq, lru_width)
        o_lru = jnp.concatenate([ol_p, ol_s], axis=0)
        u_lru = z[:, o_lru_in:o_q]
        outs_prompt[3].append(u_lru[:mp].reshape(batch, seq, lru_width)[:, seq - state_lru_conv.shape[2]:])
        outs_sample[3].append(jnp.concatenate([state_lru_conv[i, :, 1:], u_lru[mp:, None]], axis=1))
        outs_prompt[4].append(hl_p)
        outs_sample[4].append(hl_s)

        cq, ckv, ckv_b, kr, kr_b = _mla_prep(z, z_kr, cos, sin, mla_q_norm, mla_kv_norm, i,
                                             o_q // q_lora, o_kv // kv_lora, q_lora, kv_lora)
        outs_prompt[0].append(ckv[:mp].reshape(batch, seq, kv_lora))
        outs_prompt[1].append(kr[:mp].reshape(batch, seq, QK_ROPE))
        outs_sample[0].append(ckv[mp:].reshape(nseq, 1, kv_lora))
        outs_sample[1].append(kr[mp:].reshape(nseq, 1, QK_ROPE))
        w_uq = mla_w_uq[i].reshape(q_lora, heads, qk_head)
        w_uq_n = w_uq[:, :, :QK_NOPE].reshape(q_lora, heads * QK_NOPE)
        w_uq_r = w_uq[:, :, QK_NOPE:].reshape(q_lora, heads * QK_ROPE)
        qn = _mm(cq, w_uq_n, ncols=heads * QK_NOPE, tn=_tile(heads * QK_NOPE, 1024, V7X_LANES), tm=tm,
                 out_dtype=BF16, name="q_nope")
        qr = _mm(cq, w_uq_r, ncols=heads * QK_ROPE, tn=_tile(heads * QK_ROPE, 1024, V7X_LANES), tm=tm,
                 out_dtype=BF16, epilogue="rope", extra=(cos, sin), name="q_rope")
        w_kv = mla_w_ukv[i].reshape(kv_lora, heads, QK_NOPE + V_HEAD)
        w_k = w_kv[:, :, :QK_NOPE].reshape(kv_lora, heads * QK_NOPE)
        w_v = w_kv[:, :, QK_NOPE:].reshape(kv_lora, heads * V_HEAD)
        kn = _mm(ckv_b, w_k, ncols=heads * QK_NOPE, tn=_tile(heads * QK_NOPE, 1024, V7X_LANES), tm=tmp,
                 rows=mp, out_dtype=BF16, name="k_nope")
        vv = _mm(ckv_b, w_v, ncols=heads * V_HEAD, tn=_tile(heads * V_HEAD, 1024, V7X_LANES), tm=tmp,
                 rows=mp, out_dtype=BF16, name="v_up_prompt")
        om_p = _attn_prompt(qn, qr, kn, kr_b, vv, batch, seq, heads, scale)

        q_lat = _q_latent(qn, mla_w_ukv, i, mp // nseq, nseq, heads, kv_lora)
        o_lat = _paged_attention(page_table, q_lat.reshape(nseq, heads, kv_lora),
                                 qr[mp:].reshape(nseq, heads, QK_ROPE),
                                 ckv_b[mp:].reshape(nseq, 1, kv_lora), kr_b[mp:].reshape(nseq, 1, V7X_LANES),
                                 cache_ckv, cache_krope_t, i, scale)
        om_s = _v_up(o_lat.reshape(nseq, heads * kv_lora), mla_w_ukv, i, nseq, heads, kv_lora)
        o_mla = jnp.concatenate([om_p, om_s], axis=0)

        merged = _merge(o_pool, o_lru, o_mla, w_branch_pool, w_branch_lru, w_branch_mla, gate_logits, i,
                        tn=_tile(d, 512, V7X_LANES), tm=tm_small)
        mix = _mm(merged, w_out, prefix=(i,), ncols=d, tn=_tile(d, 512, V7X_LANES), tm=tm, out_dtype=F32,
                  name="out_proj")
        routed = i % 2 == 1
        x, h2, *h2_packed = _resid_norm(x, mix, norm_mix_post[i], norm_ffn_pre[i], also_packed=routed)

        j = i // 2
        if not routed:
            d_ff = ffn_w_gate.shape[2]
            act = _swiglu_up(h2, ffn_w_gate, ffn_w_up, (j,), tn=_tile(d_ff, 256, V7X_LANES), tm=tm)
            halves = 2 if (d_ff // 2) % V7X_LANES == 0 else 1
            f = None
            for kb in range(halves):
                f = _mm(act, ffn_w_down, prefix=(j,), ncols=d, tn=_tile(d, 512, V7X_LANES),
                        tm=_tile(m, 640, V7X_BF16_SUBLANES), out_dtype=F32, name="ffn_down",
                        k_block=(kb, d_ff // halves), epilogue="none" if f is None else "add",
                        extra=() if f is None else (f,))
        else:
            f = _moe(h2, h2_packed[0], moe_w_router, moe_b_router, moe_w_gate, moe_w_up, moe_w_down, j)
        g_next = norm_mix_pre[i + 1] if i + 1 < depth else None
        x, h = _resid_norm(x, f, norm_ffn_post[i], g_next)

    y_prompt = x[:mp].reshape(batch, seq, d)
    y_sample = x[mp:].reshape(nseq, 1, d)
    st = lambda lst: jnp.stack(lst)
    return (y_prompt, y_sample, st(outs_prompt[0]), st(outs_prompt[1]), st(outs_prompt[2]),
            st(outs_prompt[3]), st(outs_prompt[4]), st(outs_sample[0]), st(outs_sample[1]),
            st(outs_sample[2]), st(outs_sample[3]), st(outs_sample[4]))
```

```python
import functools

import jax
import jax.numpy as jnp
from jax import lax
from jax.experimental import pallas as pl
from jax.experimental.pallas import tpu as pltpu

BF16 = jnp.bfloat16
F32 = jnp.float32

V7X_LANES = 128
V7X_BF16_SUBLANES = 16
V7X_VMEM_LIMIT_BYTES = 56 * 1024 * 1024

POOL_WINDOWS = (2, 4, 8, 16)
LRU_C = 8.0
QK_NOPE = 128
QK_ROPE = 64
V_HEAD = 128
ROPE_BASE = 10000.0
TOP_K = 2
EPS = 1e-6
MXU_TILE = 256


def _tile(n, target, mult):
    best = None
    for t in range(mult, min(n, target) + 1, mult):
        if n % t == 0:
            best = t
    return best if best is not None else n


def _params(sem):
    return pltpu.CompilerParams(dimension_semantics=sem, vmem_limit_bytes=V7X_VMEM_LIMIT_BYTES)


def _wspec(w, k, tn, prefix, col0, kb=0):
    nlead = len(prefix)
    return pl.BlockSpec((None,) * nlead + (k, tn), lambda n, m: tuple(prefix) + (kb, n + col0))


def _wspec_t(k, tn, prefix, row0):
    nlead = len(prefix)
    return pl.BlockSpec((pl.Element(1),) * nlead + (pl.Element(tn), pl.Element(k)),
                        lambda n, m: tuple(prefix) + (pl.multiple_of(row0 + n * tn, 8), 0))


def _rms(x, g):
    return x * lax.rsqrt(jnp.mean(x * x, axis=-1, keepdims=True) + EPS) * g


def _norm_kernel(x_ref, g_ref, h_ref):
    h_ref[...] = _rms(x_ref[...], g_ref[...]).astype(h_ref.dtype)


def _pack_bf16_pairs(h):
    half = h.shape[1] // 2
    bits = lax.bitcast_convert_type(h.astype(F32), jnp.uint32)
    return (bits[:, :half] >> 16) | (bits[:, half:] & jnp.uint32(0xFFFF0000))


def _unpack_bf16_pairs(p):
    lo = lax.bitcast_convert_type(p << 16, F32)
    hi = lax.bitcast_convert_type(p & jnp.uint32(0xFFFF0000), F32)
    return jnp.concatenate([lo, hi], axis=1).astype(BF16)


def _resid_norm_kernel(x_ref, f_ref, gp_ref, gn_ref, xo_ref, h_ref, *packed_ref):
    x = x_ref[...] + _rms(f_ref[...], gp_ref[...])
    xo_ref[...] = x
    h = _rms(x, gn_ref[...]).astype(h_ref.dtype)
    h_ref[...] = h
    for p_ref in packed_ref:
        p_ref[...] = _pack_bf16_pairs(h)


def _resid_kernel(x_ref, f_ref, gp_ref, xo_ref):
    xo_ref[...] = x_ref[...] + _rms(f_ref[...], gp_ref[...])


def _norm(x, g):
    m, d = x.shape
    tm = _tile(m, 256, V7X_BF16_SUBLANES)
    row = pl.BlockSpec((tm, d), lambda i: (i, 0))
    vec = pl.BlockSpec((1, d), lambda i: (0, 0))
    return pl.pallas_call(
        _norm_kernel, grid=(m // tm,), in_specs=[row, vec], out_specs=row,
        out_shape=jax.ShapeDtypeStruct((m, d), BF16), compiler_params=_params(("arbitrary",)),
        name="rms_norm")(x, g.reshape(1, d))


def _resid_norm(x, f, g_post, g_next, also_packed=False):
    m, d = x.shape
    tm = _tile(m, 256, V7X_BF16_SUBLANES)
    row = pl.BlockSpec((tm, d), lambda i: (i, 0))
    vec = pl.BlockSpec((1, d), lambda i: (0, 0))
    if g_next is None:
        return pl.pallas_call(
            _resid_kernel, grid=(m // tm,), in_specs=[row, row, vec], out_specs=row,
            out_shape=jax.ShapeDtypeStruct((m, d), F32), compiler_params=_params(("arbitrary",)),
            name="resid")(x, f, g_post.reshape(1, d)), None
    out_specs = [row, row]
    out_shape = [jax.ShapeDtypeStruct((m, d), F32), jax.ShapeDtypeStruct((m, d), BF16)]
    if also_packed:
        out_specs.append(pl.BlockSpec((tm, d // 2), lambda i: (i, 0)))
        out_shape.append(jax.ShapeDtypeStruct((m, d // 2), jnp.uint32))
    return pl.pallas_call(
        _resid_norm_kernel, grid=(m // tm,), in_specs=[row, row, vec, vec],
        out_specs=out_specs, out_shape=out_shape,
        compiler_params=_params(("arbitrary",)),
        name="resid_norm")(x, f, g_post.reshape(1, d), g_next.reshape(1, d))


def _rope_tile(acc, cos_ref, sin_ref):
    tn = acc.shape[1]
    reps = tn // V7X_LANES
    cos = jnp.tile(cos_ref[...], (1, reps))
    sin = jnp.tile(sin_ref[...], (1, reps))
    lane = lax.broadcasted_iota(jnp.int32, acc.shape, 1) % QK_ROPE
    half = QK_ROPE // 2
    partner = jnp.where(lane < half, pltpu.roll(acc, tn - half, 1), pltpu.roll(acc, half, 1))
    return acc * cos + partner * sin


_NT = (((1,), (1,)), ((), ()))


def _mm_kernel(*refs, epilogue, w_rows):
    if epilogue == "rope":
        x_ref, w_ref, cos_ref, sin_ref, o_ref, wb_ref = refs
    elif epilogue == "add":
        x_ref, w_ref, a_ref, o_ref, wb_ref = refs
    else:
        x_ref, w_ref, o_ref, wb_ref = refs

    @pl.when(pl.program_id(1) == 0)
    def _():
        wb_ref[...] = w_ref[(0,) * (len(w_ref.shape) - 2)].astype(BF16)

    if w_rows:
        acc = lax.dot_general(x_ref[...], wb_ref[...], _NT, preferred_element_type=F32)
    else:
        acc = jnp.dot(x_ref[...], wb_ref[...], preferred_element_type=F32)
    if epilogue == "rope":
        acc = _rope_tile(acc, cos_ref, sin_ref)
    elif epilogue == "add":
        acc = a_ref[...] + acc
    o_ref[...] = acc.astype(o_ref.dtype)


def _mm(x, w, *, prefix=(), col0=0, ncols, tn, tm, out_dtype, rows=None, row0=0,
        epilogue="none", extra=(), name="mm", w_rows=False, k_block=None):
    m, k = x.shape
    kb, k = (0, k) if k_block is None else k_block
    rows = m if rows is None else rows
    assert rows % tm == 0 and ncols % tn == 0
    grid = (ncols // tn, rows // tm)
    wspec = _wspec_t(k, tn, prefix, col0) if w_rows else _wspec(w, k, tn, prefix, col0, kb)
    in_specs = [pl.BlockSpec((tm, k), lambda n, i: (i + row0, kb)), wspec]
    if epilogue == "rope":
        in_specs += [pl.BlockSpec((tm, V7X_LANES), lambda n, i: (i + row0, 0))] * 2
    elif epilogue == "add":
        in_specs += [pl.BlockSpec((tm, tn), lambda n, i: (i, n))]
    return pl.pallas_call(
        functools.partial(_mm_kernel, epilogue=epilogue, w_rows=w_rows), grid=grid, in_specs=in_specs,
        out_specs=pl.BlockSpec((tm, tn), lambda n, i: (i, n)),
        out_shape=jax.ShapeDtypeStruct((rows, ncols), out_dtype),
        scratch_shapes=[pltpu.VMEM((tn, k) if w_rows else (k, tn), BF16)],
        compiler_params=_params(("arbitrary", "arbitrary")), name=name)(x, w, *extra)


def _swiglu_kernel(x_ref, wg_ref, wu_ref, o_ref, wgb_ref, wub_ref):
    @pl.when(pl.program_id(1) == 0)
    def _():
        wgb_ref[...] = wg_ref[...].astype(BF16)
        wub_ref[...] = wu_ref[...].astype(BF16)

    x = x_ref[...]
    a = jnp.dot(x, wgb_ref[...], preferred_element_type=F32)
    b = jnp.dot(x, wub_ref[...], preferred_element_type=F32)
    o_ref[...] = (a * jax.nn.sigmoid(a) * b).astype(o_ref.dtype)


def _swiglu_up(x, wg, wu, prefix, *, tn, tm):
    m, k = x.shape
    n = wg.shape[-1]
    return pl.pallas_call(
        _swiglu_kernel, grid=(n // tn, m // tm),
        in_specs=[pl.BlockSpec((tm, k), lambda j, i: (i, 0)),
                  _wspec(wg, k, tn, prefix, 0), _wspec(wu, k, tn, prefix, 0)],
        out_specs=pl.BlockSpec((tm, tn), lambda j, i: (i, j)),
        out_shape=jax.ShapeDtypeStruct((m, n), BF16),
        scratch_shapes=[pltpu.VMEM((k, tn), BF16), pltpu.VMEM((k, tn), BF16)],
        compiler_params=_params(("arbitrary", "arbitrary")), name="swiglu_up")(x, wg, wu)


def _merge_kernel(op_ref, ol_ref, om_ref, wp_ref, wl_ref, wm_ref, gp_ref, gl_ref, gm_ref,
                  o_ref, wpb_ref, wlb_ref, wmb_ref):
    @pl.when(pl.program_id(1) == 0)
    def _():
        wpb_ref[...] = wp_ref[...].astype(BF16)
        wlb_ref[...] = wl_ref[...].astype(BF16)
        wmb_ref[...] = wm_ref[...].astype(BF16)

    acc = jax.nn.sigmoid(gp_ref[...]) * jnp.dot(op_ref[...], wpb_ref[...], preferred_element_type=F32)
    acc += jax.nn.sigmoid(gl_ref[...]) * jnp.dot(ol_ref[...], wlb_ref[...], preferred_element_type=F32)
    acc += jax.nn.sigmoid(gm_ref[...]) * jnp.dot(om_ref[...], wmb_ref[...], preferred_element_type=F32)
    o_ref[...] = acc.astype(o_ref.dtype)


def _merge(o_pool, o_lru, o_mla, wbp, wbl, wbm, gate_logits, layer, *, tn, tm):
    m = o_pool.shape[0]
    d = wbp.shape[-1]
    nb = d // tn
    lhs = lambda a: pl.BlockSpec((tm, a.shape[1]), lambda n, i: (i, 0))
    wsp = lambda w: _wspec(w, w.shape[-2], tn, (layer,), 0)
    gate = lambda g: pl.BlockSpec((tm, tn), lambda n, i: (i, n + g * nb))
    return pl.pallas_call(
        _merge_kernel, grid=(nb, m // tm),
        in_specs=[lhs(o_pool), lhs(o_lru), lhs(o_mla), wsp(wbp), wsp(wbl), wsp(wbm),
                  gate(0), gate(1), gate(2)],
        out_specs=pl.BlockSpec((tm, tn), lambda n, i: (i, n)),
        out_shape=jax.ShapeDtypeStruct((m, d), BF16),
        scratch_shapes=[pltpu.VMEM((wbp.shape[-2], tn), BF16), pltpu.VMEM((wbl.shape[-2], tn), BF16),
                        pltpu.VMEM((wbm.shape[-2], tn), BF16)],
        compiler_params=_params(("arbitrary", "arbitrary")),
        name="gated_merge")(o_pool, o_lru, o_mla, wbp, wbl, wbm, gate_logits, gate_logits, gate_logits)


def _pool_project(sums, u, cnts, pw_ref, ps_ref, group):
    outs = []
    for g in range(len(POOL_WINDOWS)):
        sl = slice(g * group, (g + 1) * group)
        pooled = sums[g][:, sl] / cnts[g] - u[:, sl]
        outs.append(jnp.dot(pooled.astype(BF16), pw_ref[g].astype(BF16), preferred_element_type=F32))
    return jnp.concatenate(outs, axis=1) * ps_ref[...]


def _pool_prompt_kernel(u_ref, halo_ref, pw_ref, ps_ref, o_ref, *, tt, group):
    t = pl.program_id(1)
    halo_rows = halo_ref.shape[0]
    halo = jnp.where(t == 0, 0.0, halo_ref[...])
    u = u_ref[...]
    full = jnp.concatenate([halo, u], axis=0)
    sums = []
    s = full
    span = 1
    for win in POOL_WINDOWS:
        while span < win:
            s = s + pltpu.roll(s, span, 0)
            span *= 2
        sums.append(s[halo_rows:])
    pos = t * tt + lax.broadcasted_iota(jnp.int32, (tt, 1), 0)
    cnts = [jnp.minimum(pos + 1, win).astype(F32) for win in POOL_WINDOWS]
    o_ref[...] = _pool_project(sums, u, cnts, pw_ref, ps_ref, group).astype(o_ref.dtype)


def _pool_prompt(z, pool_w, pool_scale, layer, batch, seq, width):
    tt = _tile(seq, 256, V7X_BF16_SUBLANES)
    halo = 16
    nt = seq // tt
    group = width // len(POOL_WINDOWS)
    return pl.pallas_call(
        functools.partial(_pool_prompt_kernel, tt=tt, group=group), grid=(batch, nt),
        in_specs=[pl.BlockSpec((tt, width), lambda b, t: (b * nt + t, 0)),
                  pl.BlockSpec((halo, width),
                               lambda b, t: (jnp.maximum((b * nt + t) * (tt // halo) - 1, 0), 0)),
                  pl.BlockSpec((None,) + pool_w.shape[1:], lambda b, t: (layer, 0, 0, 0)),
                  pl.BlockSpec((None, 1, width), lambda b, t: (layer, 0, 0))],
        out_specs=pl.BlockSpec((tt, width), lambda b, t: (b * nt + t, 0)),
        out_shape=jax.ShapeDtypeStruct((batch * seq, width), BF16),
        compiler_params=_params(("arbitrary", "arbitrary")), name="pool_prompt")(z, z, pool_w, pool_scale)


def _pool_sample_kernel(u_ref, st_ref, pw_ref, ps_ref, o_ref, *, group):
    u = u_ref[...]
    nbuf = st_ref.shape[0]
    sums = []
    s = u
    taken = 0
    for win in POOL_WINDOWS:
        while taken < win - 1:
            taken += 1
            s = s + st_ref[nbuf - taken]
        sums.append(s)
    cnts = [float(win) for win in POOL_WINDOWS]
    o_ref[...] = _pool_project(sums, u, cnts, pw_ref, ps_ref, group).astype(o_ref.dtype)


def _pool_sample(z, state_t, pool_w, pool_scale, layer, row_block, nrows, width):
    group = width // len(POOL_WINDOWS)
    return pl.pallas_call(
        functools.partial(_pool_sample_kernel, group=group), grid=(1,),
        in_specs=[pl.BlockSpec((nrows, width), lambda i: (row_block, 0)),
                  pl.BlockSpec(state_t.shape, lambda i: (0, 0, 0)),
                  pl.BlockSpec((None,) + pool_w.shape[1:], lambda i: (layer, 0, 0, 0)),
                  pl.BlockSpec((None, 1, width), lambda i: (layer, 0, 0))],
        out_specs=pl.BlockSpec((nrows, width), lambda i: (0, 0)),
        out_shape=jax.ShapeDtypeStruct((nrows, width), BF16),
        compiler_params=_params(("arbitrary",)), name="pool_sample")(z, state_t, pool_w, pool_scale)


def _lru_gates(xc, wa_ref, wx_ref, ba_ref, bx_ref, lam_ref):
    nchunk, cw, _ = wa_ref.shape
    xb = xc.astype(BF16)
    ra, rx = [], []
    for c in range(nchunk):
        xs = xb[:, c * cw:(c + 1) * cw]
        ra.append(jnp.dot(xs, wa_ref[c].astype(BF16), preferred_element_type=F32))
        rx.append(jnp.dot(xs, wx_ref[c].astype(BF16), preferred_element_type=F32))
    r = jax.nn.sigmoid(jnp.concatenate(ra, axis=1) + ba_ref[...])
    gi = jax.nn.sigmoid(jnp.concatenate(rx, axis=1) + bx_ref[...])
    nl = -lam_ref[...]
    softplus = jnp.maximum(nl, 0.0) + jnp.log1p(jnp.exp(-jnp.abs(nl)))
    log_a = -LRU_C * r * softplus
    a = jnp.exp(log_a)
    b = jnp.sqrt(jnp.tanh(-log_a) * (a * a + 1.0)) * (gi * xc)
    return a, b


def _lru_prompt_kernel(u_ref, cw_ref, cb_ref, wa_ref, wx_ref, ba_ref, bx_ref, lam_ref,
                       o_ref, hl_ref, tail_ref, h_ref, *, tc):
    t = pl.program_id(1)
    keep = tail_ref.shape[0]

    @pl.when(t == 0)
    def _():
        tail_ref[...] = jnp.zeros_like(tail_ref)
        h_ref[...] = jnp.zeros_like(h_ref)

    u = u_ref[...]
    full = jnp.concatenate([tail_ref[...], u], axis=0)
    taps = cw_ref.shape[0]
    xc = cb_ref[...] + cw_ref[taps - 1:taps, :] * u
    for k in range(1, taps):
        xc = xc + cw_ref[taps - 1 - k:taps - k, :] * pltpu.roll(full, k, 0)[keep:]
    tail_ref[...] = u[tc - keep:]

    a, b = _lru_gates(xc, wa_ref, wx_ref, ba_ref, bx_ref, lam_ref)
    row = lax.broadcasted_iota(jnp.int32, (tc, 1), 0)
    d = 1
    while d < tc:
        valid = row >= d
        a_prev = jnp.where(valid, pltpu.roll(a, d, 0), 1.0)
        b_prev = jnp.where(valid, pltpu.roll(b, d, 0), 0.0)
        b = a * b_prev + b
        a = a * a_prev
        d *= 2
    hs = a * h_ref[...] + b
    h_last = hs[tc - 1:tc]
    h_ref[...] = h_last
    hl_ref[...] = h_last
    o_ref[...] = hs.astype(o_ref.dtype)


def _lru_prompt(z, col_block, conv_w, conv_b, wa_bd, wx_bd, ba, bx, lam, layer, batch, seq, width):
    tc = _tile(seq, 256, V7X_BF16_SUBLANES)
    nt = seq // tc
    vec = pl.BlockSpec((None, 1, width), lambda b, t: (layer, 0, 0))
    bd = pl.BlockSpec(wa_bd.shape, lambda b, t: (0, 0, 0))
    o, h_last = pl.pallas_call(
        functools.partial(_lru_prompt_kernel, tc=tc), grid=(batch, nt),
        in_specs=[pl.BlockSpec((tc, width), lambda b, t: (b * nt + t, col_block)),
                  pl.BlockSpec((None,) + conv_w.shape[1:], lambda b, t: (layer, 0, 0)),
                  vec, bd, bd, vec, vec, vec],
        out_specs=[pl.BlockSpec((tc, width), lambda b, t: (b * nt + t, 0)),
                   pl.BlockSpec((None, 1, width), lambda b, t: (b, 0, 0))],
        out_shape=[jax.ShapeDtypeStruct((batch * seq, width), BF16),
                   jax.ShapeDtypeStruct((batch, 1, width), F32)],
        scratch_shapes=[pltpu.VMEM((8, width), F32), pltpu.VMEM((1, width), F32)],
        compiler_params=_params(("arbitrary", "arbitrary")),
        name="lru_prompt")(z, conv_w, conv_b, wa_bd, wx_bd, ba, bx, lam)
    return o, h_last.reshape(batch, width)


def _lru_sample_kernel(u_ref, cs_ref, h0_ref, cw_ref, cb_ref, wa_ref, wx_ref, ba_ref, bx_ref, lam_ref,
                       o_ref, hn_ref):
    taps = cw_ref.shape[0]
    xc = cb_ref[...] + cw_ref[taps - 1:taps, :] * u_ref[...]
    for k in range(taps - 1):
        xc = xc + cw_ref[k:k + 1, :] * cs_ref[k]
    a, b = _lru_gates(xc, wa_ref, wx_ref, ba_ref, bx_ref, lam_ref)
    h = a * h0_ref[...] + b
    hn_ref[...] = h
    o_ref[...] = h.astype(o_ref.dtype)


def _lru_sample(z, row_block, col_block, conv_t, h0, conv_w, conv_b, wa_bd, wx_bd, ba, bx, lam,
                layer, nrows, width):
    vec = pl.BlockSpec((None, 1, width), lambda i: (layer, 0, 0))
    bd = pl.BlockSpec(wa_bd.shape, lambda i: (0, 0, 0))
    return pl.pallas_call(
        _lru_sample_kernel, grid=(1,),
        in_specs=[pl.BlockSpec((nrows, width), lambda i: (row_block, col_block)),
                  pl.BlockSpec(conv_t.shape, lambda i: (0, 0, 0)),
                  pl.BlockSpec((None, nrows, width), lambda i: (layer, 0, 0)),
                  pl.BlockSpec((None,) + conv_w.shape[1:], lambda i: (layer, 0, 0)),
                  vec, bd, bd, vec, vec, vec],
        out_specs=[pl.BlockSpec((nrows, width), lambda i: (0, 0))] * 2,
        out_shape=[jax.ShapeDtypeStruct((nrows, width), BF16), jax.ShapeDtypeStruct((nrows, width), F32)],
        compiler_params=_params(("arbitrary",)),
        name="lru_sample")(z, conv_t, h0, conv_w, conv_b, wa_bd, wx_bd, ba, bx, lam)


def _block_diag(w, chunk):
    heads, blk, _ = w.shape
    per = chunk // blk
    eye = jnp.eye(per, dtype=w.dtype)
    wr = w.reshape(heads // per, per, blk, blk)
    return jnp.einsum("cpij,pq->cpiqj", wr, eye).reshape(heads // per, chunk, chunk)


def _mla_prep_kernel(cq_ref, ckv_ref, kr_ref, cos_ref, sin_ref, gq_ref, gkv_ref,
                     cqn_ref, ckvn_ref, ckvb_ref, krn_ref, krb_ref):
    cqn_ref[...] = _rms(cq_ref[...], gq_ref[...]).astype(cqn_ref.dtype)
    ckv = _rms(ckv_ref[...], gkv_ref[...])
    ckvn_ref[...] = ckv
    ckvb_ref[...] = ckv.astype(ckvb_ref.dtype)
    x = kr_ref[...]
    lane = lax.broadcasted_iota(jnp.int32, x.shape, 1)
    half = QK_ROPE // 2
    partner = jnp.where(lane % QK_ROPE < half, pltpu.roll(x, V7X_LANES - half, 1), pltpu.roll(x, half, 1))
    kr = jnp.where(lane < QK_ROPE, x * cos_ref[...] + partner * sin_ref[...], 0.0)
    krn_ref[...] = kr[:, :QK_ROPE]
    krb_ref[...] = (kr + pltpu.roll(kr, QK_ROPE, 1)).astype(krb_ref.dtype)


def _mla_prep(z, z_kr, cos, sin, q_norm, kv_norm, layer, q_block, kv_block, q_lora, kv_lora):
    m = z.shape[0]
    tm = _tile(m, 512, V7X_BF16_SUBLANES)
    rows = lambda w, c: pl.BlockSpec((tm, w), lambda i: (i, c))
    return pl.pallas_call(
        _mla_prep_kernel, grid=(m // tm,),
        in_specs=[rows(q_lora, q_block), rows(kv_lora, kv_block), rows(V7X_LANES, 0),
                  rows(V7X_LANES, 0), rows(V7X_LANES, 0),
                  pl.BlockSpec((None, 1, q_lora), lambda i: (layer, 0, 0)),
                  pl.BlockSpec((None, 1, kv_lora), lambda i: (layer, 0, 0))],
        out_specs=[rows(q_lora, 0), rows(kv_lora, 0), rows(kv_lora, 0), rows(QK_ROPE, 0),
                   rows(V7X_LANES, 0)],
        out_shape=[jax.ShapeDtypeStruct((m, q_lora), BF16), jax.ShapeDtypeStruct((m, kv_lora), F32),
                   jax.ShapeDtypeStruct((m, kv_lora), BF16), jax.ShapeDtypeStruct((m, QK_ROPE), F32),
                   jax.ShapeDtypeStruct((m, V7X_LANES), BF16)],
        compiler_params=_params(("arbitrary",)), name="mla_prep")(z, z, z_kr, cos, sin, q_norm, kv_norm)


ATTN_HEADS_PER_STEP = 4


def _attn_prompt_kernel(qn_ref, qr_ref, kn_ref, kr_ref, v_ref, o_ref, *, tq, tk, scale):
    qi = pl.program_id(2)
    lane = lax.broadcasted_iota(jnp.int32, (tq, V7X_LANES), 1)
    qpos = qi * tq + lax.broadcasted_iota(jnp.int32, (tq, tk), 0)
    kidx = lax.broadcasted_iota(jnp.int32, (tq, tk), 1)
    queries = []
    for hh in range(ATTN_HEADS_PER_STEP):
        pair = qr_ref[:, (hh // 2) * V7X_LANES:(hh // 2 + 1) * V7X_LANES]
        in_head = (lane >= (hh % 2) * QK_ROPE) & (lane < (hh % 2 + 1) * QK_ROPE)
        qr = jnp.where(in_head, pair, jnp.zeros_like(pair))
        queries.append(jnp.concatenate([qn_ref[:, hh * QK_NOPE:(hh + 1) * QK_NOPE], qr], axis=1))

    def body(kb, carry):
        off = pl.multiple_of(kb * tk, tk)
        kr = kr_ref[pl.ds(off, tk), :]
        visible = kidx + off <= qpos
        new = []
        for hh, (m, l, acc) in enumerate(carry):
            hs = slice(hh * QK_NOPE, (hh + 1) * QK_NOPE)
            keys = jnp.concatenate([kn_ref[pl.ds(off, tk), hs], kr], axis=1)
            s = lax.dot_general(queries[hh], keys, _NT, preferred_element_type=F32) * scale
            s = jnp.where(visible, s, -jnp.inf)
            m_new = jnp.maximum(m, jnp.max(s, axis=1, keepdims=True))
            corr = jnp.exp(m - m_new)
            p = jnp.exp(s - m_new)
            l = l * corr + jnp.sum(p, axis=1, keepdims=True)
            acc = acc * corr + jnp.dot(p.astype(BF16), v_ref[pl.ds(off, tk), hs], preferred_element_type=F32)
            new.append((m_new, l, acc))
        return tuple(new)

    init = tuple((jnp.full((tq, 1), -jnp.inf, F32), jnp.zeros((tq, 1), F32), jnp.zeros((tq, V_HEAD), F32))
                 for _ in range(ATTN_HEADS_PER_STEP))
    nkb = (qi * tq + tq + tk - 1) // tk
    final = lax.fori_loop(0, nkb, body, init)
    o_ref[...] = jnp.concatenate([acc / l for _, l, acc in final], axis=1).astype(o_ref.dtype)


def _attn_prompt(qn, qr, kn, krb, v, batch, seq, heads, scale):
    tq = _tile(seq, 256, V7X_BF16_SUBLANES)
    tk = _tile(seq, 512, tq)
    nq = seq // tq
    hps = ATTN_HEADS_PER_STEP
    wide = hps * QK_NOPE
    return pl.pallas_call(
        functools.partial(_attn_prompt_kernel, tq=tq, tk=tk, scale=scale), grid=(batch, heads // hps, nq),
        in_specs=[pl.BlockSpec((tq, wide), lambda b, h, q: (b * nq + q, h)),
                  pl.BlockSpec((tq, hps * QK_ROPE), lambda b, h, q: (b * nq + q, h)),
                  pl.BlockSpec((seq, wide), lambda b, h, q: (b, h)),
                  pl.BlockSpec((seq, V7X_LANES), lambda b, h, q: (b, 0)),
                  pl.BlockSpec((seq, wide), lambda b, h, q: (b, h))],
        out_specs=pl.BlockSpec((tq, wide), lambda b, h, q: (b * nq + q, h)),
        out_shape=jax.ShapeDtypeStruct((batch * seq, heads * V_HEAD), BF16),
        compiler_params=_params(("arbitrary", "arbitrary", "arbitrary")),
        name="attn_prompt")(qn, qr, kn, krb, v)


def _head_map_kernel(x_ref, w_ref, o_ref, *, transpose_w):
    w = w_ref[...].astype(BF16)
    dims = (((1,), (1,)), ((), ())) if transpose_w else (((1,), (0,)), ((), ()))
    o_ref[...] = lax.dot_general(x_ref[...], w, dims, preferred_element_type=F32).astype(o_ref.dtype)


def _q_latent(qn, w_ukv, layer, row_block, nrows, heads, kv_lora):
    return pl.pallas_call(
        functools.partial(_head_map_kernel, transpose_w=True), grid=(heads,),
        in_specs=[pl.BlockSpec((nrows, QK_NOPE), lambda h: (row_block, h)),
                  pl.BlockSpec((None, kv_lora, QK_NOPE), lambda h: (layer, 0, 2 * h))],
        out_specs=pl.BlockSpec((nrows, kv_lora), lambda h: (0, h)),
        out_shape=jax.ShapeDtypeStruct((nrows, heads * kv_lora), BF16),
        compiler_params=_params(("arbitrary",)), name="q_latent")(qn, w_ukv)


def _v_up(o_lat, w_ukv, layer, nrows, heads, kv_lora):
    return pl.pallas_call(
        functools.partial(_head_map_kernel, transpose_w=False), grid=(heads,),
        in_specs=[pl.BlockSpec((nrows, kv_lora), lambda h: (0, h)),
                  pl.BlockSpec((None, kv_lora, V_HEAD), lambda h: (layer, 0, 2 * h + 1))],
        out_specs=pl.BlockSpec((nrows, V_HEAD), lambda h: (0, h)),
        out_shape=jax.ShapeDtypeStruct((nrows, heads * V_HEAD), BF16),
        compiler_params=_params(("arbitrary",)), name="v_up")(o_lat, w_ukv)


PAGED_SLOTS = 3


def _paged_kernel(pt_ref, ql_ref, qr_ref, cs_ref, ks_ref, ck_hbm, kp_hbm, o_ref, ck_buf, kp_buf, sem,
                  *, layer, group, n_groups, n_seq, scale):
    b = pl.program_id(0)
    total = n_seq * n_groups
    n_slots, _, page, kv_lora = ck_buf.shape
    ahead = n_slots - 1

    def group_copies(k):
        seq = k // n_groups
        first = (k % n_groups) * group
        slot = k % n_slots
        copies = []
        for j in range(group):
            phys = pt_ref[seq, first + j]
            copies.append(pltpu.make_async_copy(ck_hbm.at[layer, phys], ck_buf.at[slot, j], sem.at[slot, 0]))
            copies.append(pltpu.make_async_copy(kp_hbm.at[layer, phys], kp_buf.at[slot, j], sem.at[slot, 1]))
        return copies

    @pl.when(b == 0)
    def _():
        for k in range(min(ahead, total)):
            for c in group_copies(k):
                c.start()

    ql = ql_ref[...]
    qr = qr_ref[...]

    def body(g, carry):
        m, l, acc = carry
        k = b * n_groups + g
        slot = k % n_slots

        @pl.when(k + ahead < total)
        def _():
            for c in group_copies(k + ahead):
                c.start()

        for c in group_copies(k):
            c.wait()

        ck = ck_buf[slot].reshape(group * page, kv_lora).astype(BF16)
        s_rope = jnp.concatenate(
            [jnp.dot(qr, kp_buf[slot, j].astype(BF16), preferred_element_type=F32) for j in range(group)],
            axis=1)
        s = (lax.dot_general(ql, ck, _NT, preferred_element_type=F32) + s_rope) * scale
        m_new = jnp.maximum(m, jnp.max(s, axis=1, keepdims=True))
        corr = jnp.exp(m - m_new)
        pr = jnp.exp(s - m_new)
        l = l * corr + jnp.sum(pr, axis=1, keepdims=True)
        acc = acc * corr + jnp.dot(pr.astype(BF16), ck, preferred_element_type=F32)
        return m_new, l, acc

    heads = ql.shape[0]
    init = (jnp.full((heads, 1), -jnp.inf, F32), jnp.zeros((heads, 1), F32),
            jnp.zeros((heads, kv_lora), F32))
    m, l, acc = lax.fori_loop(0, n_groups, body, init)

    cs = cs_ref[...].astype(F32)
    ks = ks_ref[...][:, :QK_ROPE].astype(F32)
    s = (jnp.sum(ql.astype(F32) * cs, axis=1, keepdims=True)
         + jnp.sum(qr.astype(F32) * ks, axis=1, keepdims=True)) * scale
    m_new = jnp.maximum(m, s)
    corr = jnp.exp(m - m_new)
    pr = jnp.exp(s - m_new)
    l_fin = l * corr + pr
    acc_fin = acc * corr + pr.astype(BF16).astype(F32) * cs
    o_ref[...] = (acc_fin / l_fin).astype(o_ref.dtype)


def _paged_attention(page_table, q_lat, q_rope, ckv_self, kr_self, cache_ckv, cache_krope_t, layer, scale):
    nseq, heads, kv_lora = q_lat.shape
    n_pages = page_table.shape[1]
    page = cache_ckv.shape[2]
    group = _tile(n_pages, 16, 1)
    n_groups = n_pages // group
    seq_spec = lambda a: pl.BlockSpec((None,) + a.shape[1:], lambda b, pt: (b, 0, 0))
    grid_spec = pltpu.PrefetchScalarGridSpec(
        num_scalar_prefetch=1, grid=(nseq,),
        in_specs=[seq_spec(q_lat), seq_spec(q_rope), seq_spec(ckv_self), seq_spec(kr_self),
                  pl.BlockSpec(memory_space=pl.ANY), pl.BlockSpec(memory_space=pl.ANY)],
        out_specs=pl.BlockSpec((None, heads, kv_lora), lambda b, pt: (b, 0, 0)),
        scratch_shapes=[pltpu.VMEM((PAGED_SLOTS, group, page, kv_lora), F32),
                        pltpu.VMEM((PAGED_SLOTS, group, QK_ROPE, page), F32),
                        pltpu.SemaphoreType.DMA((PAGED_SLOTS, 2))])
    return pl.pallas_call(
        functools.partial(_paged_kernel, layer=layer, group=group, n_groups=n_groups, n_seq=nseq,
                          scale=scale),
        grid_spec=grid_spec,
        out_shape=jax.ShapeDtypeStruct((nseq, heads, kv_lora), BF16),
        compiler_params=_params(("arbitrary",)),
        name="paged_attention")(page_table, q_lat, q_rope, ckv_self, kr_self, cache_ckv, cache_krope_t)


def _router_kernel(x_ref, w_ref, b_ref, idx_ref, wgt_ref, rank_ref, cnt_ref, carry_ref):
    @pl.when(pl.program_id(0) == 0)
    def _():
        carry_ref[...] = jnp.zeros_like(carry_ref)

    logits = jnp.dot(x_ref[...], w_ref[...].astype(BF16), preferred_element_type=F32) + b_ref[...]
    tm, ne = logits.shape
    idx = lax.broadcasted_iota(jnp.int32, logits.shape, 1)
    v1 = jnp.max(logits, axis=1, keepdims=True)
    i1 = jnp.min(jnp.where(logits == v1, idx, ne), axis=1, keepdims=True)
    rest = jnp.where(idx == i1, -jnp.inf, logits)
    v2 = jnp.max(rest, axis=1, keepdims=True)
    i2 = jnp.min(jnp.where(rest == v2, idx, ne), axis=1, keepdims=True)
    e2 = jnp.exp(v2 - v1)
    denom = 1.0 + e2

    sel = ((idx == i1) | (idx == i2)).astype(F32)
    earlier = (lax.broadcasted_iota(jnp.int32, (tm, tm), 1)
               < lax.broadcasted_iota(jnp.int32, (tm, tm), 0)).astype(BF16)
    rank_all = jnp.dot(earlier, sel.astype(BF16), preferred_element_type=F32) + carry_ref[...]
    rank1 = jnp.sum(jnp.where(idx == i1, rank_all, 0.0), axis=1, keepdims=True)
    rank2 = jnp.sum(jnp.where(idx == i2, rank_all, 0.0), axis=1, keepdims=True)
    carry = carry_ref[...] + jnp.sum(sel, axis=0, keepdims=True)
    carry_ref[...] = carry
    cnt_ref[...] = carry.astype(jnp.int32)

    first = lax.broadcasted_iota(jnp.int32, (tm, TOP_K), 1) == 0
    idx_ref[...] = jnp.where(first, i1, i2)
    wgt_ref[...] = jnp.where(first, 1.0 / denom, e2 / denom)
    rank_ref[...] = jnp.where(first, rank1, rank2).astype(jnp.int32)


def _router(x, w_router, b_router, j):
    m, d = x.shape
    ne = w_router.shape[-1]
    tm = _tile(m, 520, V7X_BF16_SUBLANES)
    pair = pl.BlockSpec((tm, TOP_K), lambda i: (i, 0))
    return pl.pallas_call(
        _router_kernel, grid=(m // tm,),
        in_specs=[pl.BlockSpec((tm, d), lambda i: (i, 0)),
                  pl.BlockSpec((None, d, ne), lambda i: (j, 0, 0)),
                  pl.BlockSpec((None, 1, ne), lambda i: (j, 0, 0))],
        out_specs=[pair, pair, pair, pl.BlockSpec((1, ne), lambda i: (0, 0))],
        out_shape=[jax.ShapeDtypeStruct((m, TOP_K), jnp.int32), jax.ShapeDtypeStruct((m, TOP_K), F32),
                   jax.ShapeDtypeStruct((m, TOP_K), jnp.int32), jax.ShapeDtypeStruct((1, ne), jnp.int32)],
        scratch_shapes=[pltpu.VMEM((1, ne), F32)],
        compiler_params=_params(("arbitrary",)), name="router_top2")(x, w_router, b_router)


def _row_copy(src_hbm, src_row, dst, dst_row, sem):
    return pltpu.make_async_copy(src_hbm.at[pl.ds(src_row, 1)], dst.at[pl.ds(dst_row, 1)], sem)


def _dispatch_kernel(pos_ref, x_hbm, o_ref, inv_ref, buf, sem, *, n_tokens, tg):
    j = pl.program_id(0)
    nj = pl.num_programs(0)

    def start_tile(tile, slot):
        def body(pair, c):
            for prio in range(2):
                r = 2 * pair + prio
                _row_copy(x_hbm, inv_ref[tile * tg + r], buf.at[slot], r, sem.at[slot]).start(priority=prio)
            return c
        lax.fori_loop(0, tg // 2, body, 0)

    @pl.when(j == 0)
    def _():
        def clear(p, c):
            inv_ref[p] = 0
            return c
        lax.fori_loop(0, nj * tg, clear, 0)

        def fill(t, c):
            for k in range(TOP_K):
                inv_ref[pos_ref[TOP_K * t + k]] = t
            return c
        lax.fori_loop(0, n_tokens, fill, 0)
        start_tile(0, 0)

    slot = j % 2

    @pl.when(j + 1 < nj)
    def _():
        start_tile(j + 1, 1 - slot)

    def wait(r, c):
        _row_copy(x_hbm, 0, buf.at[slot], r, sem.at[slot]).wait()
        return c
    lax.fori_loop(0, tg, wait, 0)
    o_ref[...] = _unpack_bf16_pairs(buf[slot])


def _dispatch(pos_flat, x_packed, n_slots, tg):
    m, half = x_packed.shape
    grid_spec = pltpu.PrefetchScalarGridSpec(
        num_scalar_prefetch=1, grid=(n_slots // tg,),
        in_specs=[pl.BlockSpec(memory_space=pl.ANY)],
        out_specs=pl.BlockSpec((tg, 2 * half), lambda j, pos: (j, 0)),
        scratch_shapes=[pltpu.SMEM((n_slots,), jnp.int32), pltpu.VMEM((2, tg, half), jnp.uint32),
                        pltpu.SemaphoreType.DMA((2,))])
    return pl.pallas_call(
        functools.partial(_dispatch_kernel, n_tokens=m, tg=tg), grid_spec=grid_spec,
        out_shape=jax.ShapeDtypeStruct((n_slots, 2 * half), BF16),
        compiler_params=_params(("arbitrary",)), name="moe_dispatch")(pos_flat, x_packed)


def _combine_kernel(pos_ref, w_ref, y_hbm, o_ref, buf, sem, *, tc):
    i = pl.program_id(0)
    ni = pl.num_programs(0)

    def start_tile(tile, slot):
        def body(r, c):
            for k in range(TOP_K):
                _row_copy(y_hbm, pos_ref[TOP_K * (tile * tc + r) + k], buf.at[slot, k], r,
                          sem.at[slot]).start(priority=k)
            return c
        lax.fori_loop(0, tc, body, 0)

    @pl.when(i == 0)
    def _():
        start_tile(0, 0)

    slot = i % 2

    @pl.when(i + 1 < ni)
    def _():
        start_tile(i + 1, 1 - slot)

    def wait(r, c):
        for k in range(TOP_K):
            _row_copy(y_hbm, 0, buf.at[slot, k], r, sem.at[slot]).wait()
        return c
    lax.fori_loop(0, tc, wait, 0)
    w = w_ref[...]
    o_ref[...] = w[:, 0:1] * buf[slot, 0] + w[:, 1:2] * buf[slot, 1]


def _combine(pos_flat, weights, y, tc):
    m = weights.shape[0]
    d = y.shape[1]
    grid_spec = pltpu.PrefetchScalarGridSpec(
        num_scalar_prefetch=1, grid=(m // tc,),
        in_specs=[pl.BlockSpec((tc, TOP_K), lambda i, pos: (i, 0)), pl.BlockSpec(memory_space=pl.ANY)],
        out_specs=pl.BlockSpec((tc, d), lambda i, pos: (i, 0)),
        scratch_shapes=[pltpu.VMEM((2, TOP_K, tc, d), F32), pltpu.SemaphoreType.DMA((2,))])
    return pl.pallas_call(
        functools.partial(_combine_kernel, tc=tc), grid_spec=grid_spec,
        out_shape=jax.ShapeDtypeStruct((m, d), F32),
        compiler_params=_params(("arbitrary",)), name="moe_combine")(pos_flat, weights, y)


def _new_expert(te_ref, j):
    return (j == 0) | (te_ref[j] != te_ref[jnp.maximum(j - 1, 0)])


def _expert_up_kernel(te_ref, nv_ref, x_ref, wg_ref, wu_ref, o_ref, wgb_ref, wub_ref):
    j = pl.program_id(1)
    used = j < nv_ref[0]

    @pl.when(used & _new_expert(te_ref, j))
    def _():
        wgb_ref[...] = wg_ref[...].astype(BF16)
        wub_ref[...] = wu_ref[...].astype(BF16)

    @pl.when(used)
    def _():
        x = x_ref[...]
        a = jnp.dot(x, wgb_ref[...], preferred_element_type=F32)
        b = jnp.dot(x, wub_ref[...], preferred_element_type=F32)
        o_ref[...] = (a * jax.nn.sigmoid(a) * b).astype(o_ref.dtype)

    @pl.when(jnp.logical_not(used))
    def _():
        o_ref[...] = jnp.zeros_like(o_ref)


def _expert_down_kernel(te_ref, nv_ref, x_ref, w_ref, o_ref, wb_ref):
    j = pl.program_id(1)
    used = j < nv_ref[0]

    @pl.when(used & _new_expert(te_ref, j))
    def _():
        wb_ref[...] = w_ref[...].astype(BF16)

    @pl.when(used)
    def _():
        o_ref[...] = jnp.dot(x_ref[...], wb_ref[...], preferred_element_type=F32)

    @pl.when(jnp.logical_not(used))
    def _():
        o_ref[...] = jnp.zeros_like(o_ref)


def _expert_matmul(body, tile_expert, n_used, x, weights, layer, out_dtype, tn, tm, name):
    rows, k = x.shape
    n = weights[0].shape[-1]
    last = lambda j, nv: jnp.minimum(j, nv[0] - 1)
    wspec = pl.BlockSpec((None, None, k, tn), lambda c, j, te, nv: (layer, te[last(j, nv)], 0, c))
    grid_spec = pltpu.PrefetchScalarGridSpec(
        num_scalar_prefetch=2, grid=(n // tn, rows // tm),
        in_specs=[pl.BlockSpec((tm, k), lambda c, j, te, nv: (last(j, nv), 0))] + [wspec] * len(weights),
        out_specs=pl.BlockSpec((tm, tn), lambda c, j, te, nv: (j, c)),
        scratch_shapes=[pltpu.VMEM((k, tn), BF16)] * len(weights))
    return pl.pallas_call(
        body, grid_spec=grid_spec, out_shape=jax.ShapeDtypeStruct((rows, n), out_dtype),
        compiler_params=_params(("arbitrary", "arbitrary")), name=name)(tile_expert, n_used, x, *weights)


def _moe(h_bf16, h_packed, w_router, b_router, w_gate, w_up, w_down, j):
    m, d = h_bf16.shape
    ne = w_router.shape[-1]
    tm_e = 2 * MXU_TILE
    idx, wgt, rank, counts = _router(h_bf16, w_router, b_router, j)
    padded = (counts[0] + tm_e - 1) // tm_e * tm_e
    ends = jnp.cumsum(padded)
    starts = ends - padded
    n_tiles = -(-(TOP_K * m + ne * (tm_e - 1)) // tm_e)
    pos_flat = (starts[idx] + rank).reshape(-1)
    tile_expert = jnp.minimum(
        jnp.searchsorted(ends, jnp.arange(n_tiles, dtype=jnp.int32) * tm_e, side="right"), ne - 1
    ).astype(jnp.int32)
    n_used = (ends[-1:] // tm_e).astype(jnp.int32)
    xs = _dispatch(pos_flat, h_packed, n_tiles * tm_e, MXU_TILE)
    act = _expert_matmul(_expert_up_kernel, tile_expert, n_used, xs, (w_gate, w_up), j, BF16,
                         _tile(w_gate.shape[-1], 512, V7X_LANES), tm_e, "expert_up")
    ys = _expert_matmul(_expert_down_kernel, tile_expert, n_used, act, (w_down,), j, F32,
                        _tile(d, 512, V7X_LANES), tm_e, "expert_down")
    return _combine(pos_flat, wgt, ys, _tile(m, 256, 8))


def _rope_tables(positions):
    half = QK_ROPE // 2
    inv = ROPE_BASE ** (-jnp.arange(half, dtype=F32) / half)
    ang = positions.astype(F32)[:, None] * inv
    cos, sin = jnp.cos(ang), jnp.sin(ang)
    reps = V7X_LANES // QK_ROPE
    return (jnp.tile(jnp.concatenate([cos, cos], axis=1), (1, reps)),
            jnp.tile(jnp.concatenate([-sin, sin], axis=1), (1, reps)))


def kernel(x_prompt, x_sample, cache_ckv, cache_krope, state_pool, state_lru_conv, state_lru_h, page_table, norm_mix_pre, norm_mix_post, norm_ffn_pre, norm_ffn_post, w_in, pool_w, pool_scale, lru_conv_w, lru_conv_b, lru_wa, lru_ba, lru_wx, lru_bx, lru_lambda, mla_q_norm, mla_w_uq, mla_kv_norm, mla_w_ukv, w_branch_pool, w_branch_lru, w_branch_mla, w_out, ffn_w_gate, ffn_w_up, ffn_w_down, moe_w_router, moe_b_router, moe_w_gate, moe_w_up, moe_w_down):
    batch, seq, d = x_prompt.shape
    nseq, dec_seq, _ = x_sample.shape
    assert dec_seq == 1
    depth = w_in.shape[0]
    pool_width = pool_scale.shape[1]
    lru_width = lru_lambda.shape[1]
    q_lora = mla_q_norm.shape[1]
    kv_lora = mla_kv_norm.shape[1]
    qk_head = QK_NOPE + QK_ROPE
    heads = mla_w_uq.shape[2] // qk_head
    past_len = page_table.shape[1] * cache_ckv.shape[2]
    scale = float(qk_head) ** -0.5
    mp = batch * seq
    m = mp + nseq
    o_lru_in = pool_width
    o_q = o_lru_in + lru_width
    o_kv = o_q + q_lora
    o_kr = o_kv + kv_lora
    o_gate = o_kr + QK_ROPE
    assert pool_width == lru_width == q_lora and o_kv % kv_lora == 0 and o_kr % V7X_LANES == 0
    assert mp % nseq == 0

    vec3 = lambda a: a.reshape(a.shape[0], 1, a.shape[1])
    pool_scale, lru_conv_b, lru_ba, lru_bx, lru_lambda, mla_q_norm, mla_kv_norm, moe_b_router = map(
        vec3, (pool_scale, lru_conv_b, lru_ba, lru_bx, lru_lambda, mla_q_norm, mla_kv_norm, moe_b_router))

    tm = _tile(m, 1040, V7X_BF16_SUBLANES)
    tm_small = _tile(m, 520, V7X_BF16_SUBLANES)
    tmp = _tile(mp, 1024, V7X_BF16_SUBLANES)
    tn = _tile(o_kr, 512, V7X_LANES)

    x = jnp.concatenate([x_prompt.reshape(mp, d), x_sample.reshape(nseq, d)], axis=0)
    positions = jnp.concatenate([jnp.tile(jnp.arange(seq), batch), jnp.full((nseq,), past_len)])
    cos, sin = _rope_tables(positions)
    w_in_t = jnp.swapaxes(w_in, 1, 2)
    cache_krope_t = jnp.swapaxes(cache_krope, 2, 3)

    outs_prompt = [[] for _ in range(5)]
    outs_sample = [[] for _ in range(5)]
    h = _norm(x, norm_mix_pre[0])
    for i in range(depth):
        z = _mm(h, w_in_t, prefix=(i,), ncols=o_kr, tn=tn, tm=tm, out_dtype=F32, name="in_proj",
                w_rows=True)
        z_kr = _mm(h, w_in_t, prefix=(i,), col0=o_kr, ncols=V7X_LANES, tn=V7X_LANES, tm=tm,
                   out_dtype=F32, name="in_proj_kr", w_rows=True)
        gate_logits = _mm(h, w_in_t, prefix=(i,), col0=o_gate, ncols=3 * d, tn=_tile(d, 512, V7X_LANES),
                          tm=tm, out_dtype=F32, name="in_proj_gates", w_rows=True)

        op_p = _pool_prompt(z, pool_w, pool_scale, i, batch, seq, pool_width)
        op_s = _pool_sample(z, state_pool[i].swapaxes(0, 1), pool_w, pool_scale, i, mp // nseq, nseq,
                            pool_width)
        o_pool = jnp.concatenate([op_p, op_s], axis=0)
        u_pool_p = z[:mp, :pool_width].reshape(batch, seq, pool_width)
        outs_prompt[2].append(u_pool_p[:, seq - state_pool.shape[2]:])
        outs_sample[2].append(jnp.concatenate([state_pool[i, :, 1:], z[mp:, None, :pool_width]], axis=1))

        chunk = min(MXU_TILE, lru_width)
        wa_bd = _block_diag(lru_wa[i], chunk)
        wx_bd = _block_diag(lru_wx[i], chunk)
        ol_p, hl_p = _lru_prompt(z, o_lru_in // lru_width, lru_conv_w, lru_conv_b, wa_bd, wx_bd,
                                 lru_ba, lru_bx, lru_lambda, i, batch, seq, lru_width)
        ol_s, hl_s = _lru_sample(z, mp // nseq, o_lru_in // lru_width, state_lru_conv[i].swapaxes(0, 1),
                                 state_lru_h, lru_conv_w, lru_conv_b, wa_bd, wx_bd, lru_ba, lru_bx,
                                 lru_lambda, i, nseq, lru_width)
        o_lru = jnp.concatenate([ol_p, ol_s], axis=0)
        u_lru = z[:, o_lru_in:o_q]
        outs_prompt[3].append(u_lru[:mp].reshape(batch, seq, lru_width)[:, seq - state_lru_conv.shape[2]:])
        outs_sample[3].append(jnp.concatenate([state_lru_conv[i, :, 1:], u_lru[mp:, None]], axis=1))
        outs_prompt[4].append(hl_p)
        outs_sample[4].append(hl_s)

        cq, ckv, ckv_b, kr, kr_b = _mla_prep(z, z_kr, cos, sin, mla_q_norm, mla_kv_norm, i,
                                             o_q // q_lora, o_kv // kv_lora, q_lora, kv_lora)
        outs_prompt[0].append(ckv[:mp].reshape(batch, seq, kv_lora))
        outs_prompt[1].append(kr[:mp].reshape(batch, seq, QK_ROPE))
        outs_sample[0].append(ckv[mp:].reshape(nseq, 1, kv_lora))
        outs_sample[1].append(kr[mp:].reshape(nseq, 1, QK_ROPE))
        w_uq = mla_w_uq[i].reshape(q_lora, heads, qk_head)
        w_uq_n = w_uq[:, :, :QK_NOPE].reshape(q_lora, heads * QK_NOPE)
        w_uq_r = w_uq[:, :, QK_NOPE:].reshape(q_lora, heads * QK_ROPE)
        qn = _mm(cq, w_uq_n, ncols=heads * QK_NOPE, tn=_tile(heads * QK_NOPE, 1024, V7X_LANES), tm=tm,
                 out_dtype=BF16, name="q_nope")
        qr = _mm(cq, w_uq_r, ncols=heads * QK_ROPE, tn=_tile(heads * QK_ROPE, 1024, V7X_LANES), tm=tm,
                 out_dtype=BF16, epilogue="rope", extra=(cos, sin), name="q_rope")
        w_kv = mla_w_ukv[i].reshape(kv_lora, heads, QK_NOPE + V_HEAD)
        w_k = w_kv[:, :, :QK_NOPE].reshape(kv_lora, heads * QK_NOPE)
        w_v = w_kv[:, :, QK_NOPE:].reshape(kv_lora, heads * V_HEAD)
        kn = _mm(ckv_b, w_k, ncols=heads * QK_NOPE, tn=_tile(heads * QK_NOPE, 1024, V7X_LANES), tm=tmp,
                 rows=mp, out_dtype=BF16, name="k_nope")
        vv = _mm(ckv_b, w_v, ncols=heads * V_HEAD, tn=_tile(heads * V_HEAD, 1024, V7X_LANES), tm=tmp,
                 rows=mp, out_dtype=BF16, name="v_up_prompt")
        om_p = _attn_prompt(qn, qr, kn, kr_b, vv, batch, seq, heads, scale)

        q_lat = _q_latent(qn, mla_w_ukv, i, mp // nseq, nseq, heads, kv_lora)
        o_lat = _paged_attention(page_table, q_lat.reshape(nseq, heads, kv_lora),
                                 qr[mp:].reshape(nseq, heads, QK_ROPE),
                                 ckv_b[mp:].reshape(nseq, 1, kv_lora), kr_b[mp:].reshape(nseq, 1, V7X_LANES),
                                 cache_ckv, cache_krope_t, i, scale)
        om_s = _v_up(o_lat.reshape(nseq, heads * kv_lora), mla_w_ukv, i, nseq, heads, kv_lora)
        o_mla = jnp.concatenate([om_p, om_s], axis=0)

        merged = _merge(o_pool, o_lru, o_mla, w_branch_pool, w_branch_lru, w_branch_mla, gate_logits, i,
                        tn=_tile(d, 512, V7X_LANES), tm=tm_small)
        mix = _mm(merged, w_out, prefix=(i,), ncols=d, tn=_tile(d, 512, V7X_LANES), tm=tm, out_dtype=F32,
                  name="out_proj")
        routed = i % 2 == 1
        x, h2, *h2_packed = _resid_norm(x, mix, norm_mix_post[i], norm_ffn_pre[i], also_packed=routed)

        j = i // 2
        if not routed:
            d_ff = ffn_w_gate.shape[2]
            act = _swiglu_up(h2, ffn_w_gate, ffn_w_up, (j,), tn=_tile(d_ff, 256, V7X_LANES), tm=tm)
            halves = 2 if (d_ff // 2) % V7X_LANES == 0 else 1
            f = None
            for kb in range(halves):
                f = _mm(act, ffn_w_down, prefix=(j,), ncols=d, tn=_tile(d, 512, V7X_LANES),
                        tm=_tile(m, 640, V7X_BF16_SUBLANES), out_dtype=F32, name="ffn_down",
                        k_block=(kb, d_ff // halves), epilogue="none" if f is None else "add",
                        extra=() if f is None else (f,))
        else:
            f = _moe(h2, h2_packed[0], moe_w_router, moe_b_router, moe_w_gate, moe_w_up, moe_w_down, j)
        g_next = norm_mix_pre[i + 1] if i + 1 < depth else None
        x, h = _resid_norm(x, f, norm_ffn_post[i], g_next)

    y_prompt = x[:mp].reshape(batch, seq, d)
    y_sample = x[mp:].reshape(nseq, 1, d)
    st = lambda lst: jnp.stack(lst)
    return (y_prompt, y_sample, st(outs_prompt[0]), st(outs_prompt[1]), st(outs_prompt[2]),
            st(outs_prompt[3]), st(outs_prompt[4]), st(outs_sample[0]), st(outs_sample[1]),
            st(outs_sample[2]), st(outs_sample[3]), st(outs_sample[4]))
```
